```python
import jax, jax.numpy as jnp
from jax import lax
import numpy as np

D_MODEL = 2048
BATCH = 2
SEQ = 8192
DEPTH = 4

CHUNK = 64
HEAD_DIM = 128
MIX_WIDTH = D_MODEL
N_HEADS_FOX = MIX_WIDTH // (2 * HEAD_DIM)
N_HEADS_SB = MIX_WIDTH // (2 * HEAD_DIM)
WIDTH_FOX = N_HEADS_FOX * HEAD_DIM
WIDTH_SB = N_HEADS_SB * HEAD_DIM
IN_COLS = 3 * WIDTH_FOX + N_HEADS_FOX + 3 * WIDTH_SB
D_FF = ((8 * D_MODEL // 3 + 255) // 256) * 256
CONV_WIDTH = 3
Q_BLOCK = 128
EPS = 1e-6

kernel_name = "fox_stickbreak_hybrid_convffn"


def rms_norm(x, g):
    x32 = x.astype(jnp.float32)
    y = x32 * lax.rsqrt(jnp.mean(x32 * x32, axis=-1, keepdims=True) + EPS)
    return (y * g.astype(jnp.float32)).astype(x.dtype)


def to_heads(t, n_heads):
    b, s, _ = t.shape
    return t.reshape(b, s, n_heads, HEAD_DIM).transpose(0, 2, 1, 3)


def from_heads(t):
    b, h, s, d = t.shape
    return t.transpose(0, 2, 1, 3).reshape(b, s, h * d)


def to_blocks(t):
    b, h, s = t.shape[:3]
    nb = s // Q_BLOCK
    t = t.reshape((b, h, nb, Q_BLOCK) + t.shape[3:])
    return jnp.moveaxis(t, 2, 0)


def from_blocks(t):
    nb, b, h, qb, d = t.shape
    return jnp.moveaxis(t, 0, 2).reshape(b, h, nb * qb, d)


def forgetting_attention(q, k, v, c):
    s_len = k.shape[2]
    scale = HEAD_DIM ** -0.5
    k_pos = jnp.arange(s_len)
    k32 = k.astype(jnp.float32)
    c32 = c.astype(jnp.float32)

    def block(args):
        qi, ci, bi = args
        logits = jnp.einsum('bhqd,bhkd->bhqk', qi.astype(jnp.float32), k32) * scale
        logits = logits + ci[..., None] - c32[:, :, None, :]
        q_pos = bi * Q_BLOCK + jnp.arange(Q_BLOCK)
        mask = k_pos[None, :] <= q_pos[:, None]
        logits = jnp.where(mask, logits, -jnp.inf)
        p = jax.nn.softmax(logits, axis=-1)
        return jnp.einsum('bhqk,bhkd->bhqd', p.astype(v.dtype), v)

    nb = s_len // Q_BLOCK
    out = lax.map(block, (to_blocks(q), to_blocks(c32), jnp.arange(nb)))
    return from_blocks(out)


def stick_breaking_attention(q, k, v):
    s_len = k.shape[2]
    scale = HEAD_DIM ** -0.5
    k_pos = jnp.arange(s_len)
    k32 = k.astype(jnp.float32)

    def block(args):
        qi, bi = args
        z = jnp.einsum('bhqd,bhkd->bhqk', qi.astype(jnp.float32), k32) * scale
        q_pos = bi * Q_BLOCK + jnp.arange(Q_BLOCK)
        mask = k_pos[None, :] < q_pos[:, None]
        log_beta = jax.nn.log_sigmoid(z)
        log_one_minus = jnp.where(mask, log_beta - z, 0.0)
        key_axis = log_one_minus.ndim - 1
        suffix = lax.cumsum(log_one_minus, axis=key_axis, reverse=True) - log_one_minus
        weights = jnp.where(mask, jnp.exp(log_beta + suffix), 0.0)
        return jnp.einsum('bhqk,bhkd->bhqd', weights.astype(v.dtype), v)

    nb = s_len // Q_BLOCK
    out = lax.map(block, (to_blocks(q), jnp.arange(nb)))
    return from_blocks(out)


def hybrid_mixer(h, w_in_l, b_forget_l, q_norm_l, k_norm_l, out_norm_fox_l, out_norm_sb_l, w_out_l):
    proj = h @ w_in_l
    splits = [WIDTH_FOX, 2 * WIDTH_FOX, 3 * WIDTH_FOX, 3 * WIDTH_FOX + N_HEADS_FOX,
              3 * WIDTH_FOX + N_HEADS_FOX + WIDTH_SB, 3 * WIDTH_FOX + N_HEADS_FOX + 2 * WIDTH_SB]
    qa, ka, va, fa, qb, kb, vb = jnp.split(proj, splits, axis=-1)

    qa = rms_norm(to_heads(qa, N_HEADS_FOX), q_norm_l)
    ka = rms_norm(to_heads(ka, N_HEADS_FOX), k_norm_l)
    log_f = jax.nn.log_sigmoid((fa + b_forget_l).astype(jnp.float32))
    c = jnp.cumsum(log_f, axis=1).transpose(0, 2, 1)
    out_a = forgetting_attention(qa, ka, to_heads(va, N_HEADS_FOX), c)

    out_b = stick_breaking_attention(to_heads(qb, N_HEADS_SB), to_heads(kb, N_HEADS_SB),
                                     to_heads(vb, N_HEADS_SB))

    merged = jnp.concatenate([rms_norm(from_heads(out_a), out_norm_fox_l),
                              rms_norm(from_heads(out_b), out_norm_sb_l)], axis=-1)
    return merged @ w_out_l


def conv_ffn(h, w_up_l, conv_w_l, conv_b_l, w_down_l):
    u = h @ w_up_l
    c = u.shape[-1]
    u = lax.conv_general_dilated(
        u, conv_w_l[:, None, :].astype(u.dtype), window_strides=(1,),
        padding=[(CONV_WIDTH - 1, 0)], dimension_numbers=('NWC', 'WIO', 'NWC'),
        feature_group_count=c) + conv_b_l
    gate, val = jnp.split(u, 2, axis=-1)
    return (jax.nn.silu(gate) * val) @ w_down_l


def setup_inputs(seed: int = 0) -> dict:
    key = jax.random.key(seed)
    ks = jax.random.split(key, 14)
    nrm = jax.random.normal
    f32 = jnp.float32
    return {
        "x": nrm(ks[0], (BATCH, SEQ, D_MODEL), f32),
        "attn_norm": 1.0 + 0.02 * nrm(ks[1], (DEPTH, D_MODEL), f32),
        "w_in": nrm(ks[2], (DEPTH, D_MODEL, IN_COLS), f32) * D_MODEL ** -0.5,
        "b_forget": 3.0 + 0.5 * nrm(ks[3], (DEPTH, N_HEADS_FOX), f32),
        "q_norm": 1.0 + 0.02 * nrm(ks[4], (DEPTH, HEAD_DIM), f32),
        "k_norm": 1.0 + 0.02 * nrm(ks[5], (DEPTH, HEAD_DIM), f32),
        "out_norm_fox": 1.0 + 0.02 * nrm(ks[6], (DEPTH, WIDTH_FOX), f32),
        "out_norm_sb": 1.0 + 0.02 * nrm(ks[7], (DEPTH, WIDTH_SB), f32),
        "w_out": nrm(ks[8], (DEPTH, MIX_WIDTH, D_MODEL), f32) * MIX_WIDTH ** -0.5,
        "ffn_norm": 1.0 + 0.02 * nrm(ks[9], (DEPTH, D_MODEL), f32),
        "w_up": nrm(ks[10], (DEPTH, D_MODEL, 2 * D_FF), f32) * D_MODEL ** -0.5,
        "conv_w": nrm(ks[11], (DEPTH, CONV_WIDTH, 2 * D_FF), f32) * CONV_WIDTH ** -0.5,
        "conv_b": 0.02 * nrm(ks[12], (DEPTH, 2 * D_FF), f32),
        "w_down": nrm(ks[13], (DEPTH, D_FF, D_MODEL), f32) * D_FF ** -0.5,
    }


def reference(x, attn_norm, w_in, b_forget, q_norm, k_norm, out_norm_fox, out_norm_sb,
              w_out, ffn_norm, w_up, conv_w, conv_b, w_down):
    for layer in range(DEPTH):
        h = rms_norm(x, attn_norm[layer])
        x = x + hybrid_mixer(h, w_in[layer], b_forget[layer], q_norm[layer], k_norm[layer],
                             out_norm_fox[layer], out_norm_sb[layer], w_out[layer])
        h = rms_norm(x, ffn_norm[layer])
        x = x + conv_ffn(h, w_up[layer], conv_w[layer], conv_b[layer], w_down[layer])
    return x
```

```python
import functools

import jax
import jax.numpy as jnp
from jax import lax
from jax.experimental import pallas as pl
from jax.experimental.pallas import tpu as pltpu

EPS = 1e-6
LANES = 128
SUBLANES = 8
MXU_DIM = 256
VMEM_LIMIT_BYTES = 56 * 1024 * 1024
MASK_VALUE = -1e30

F32 = jnp.float32
BF16 = jnp.bfloat16
_NT = (((1,), (1,)), ((), ()))


def _tile(dim, want):
    t = min(dim, want)
    while dim % t:
        t -= 1
    return t


def _params(*sem):
    return pltpu.CompilerParams(dimension_semantics=sem, vmem_limit_bytes=VMEM_LIMIT_BYTES)


def _rms(x, gain):
    ms = jnp.mean(x * x, axis=-1, keepdims=True)
    return x * lax.rsqrt(ms + EPS) * gain


def _in_proj_kernel(x_ref, g_ref, w_ref, cg_ref, wf_ref, o_ref, f_ref, h_ref, *, norm_tiles, head_dim):
    j = pl.program_id(1)

    @pl.when(j == 0)
    def _():
        h = _rms(x_ref[...], g_ref[...]).astype(BF16)
        h_ref[...] = h
        f_ref[...] = jnp.dot(h, wf_ref[...], preferred_element_type=F32)

    acc = jnp.dot(h_ref[...], w_ref[...], preferred_element_type=F32)
    tn = acc.shape[1]

    @pl.when(j < norm_tiles)
    def _():
        for g in range(tn // head_dim):
            sl = slice(g * head_dim, (g + 1) * head_dim)
            o_ref[:, sl] = _rms(acc[:, sl], cg_ref[:, sl]).astype(o_ref.dtype)

    @pl.when(j >= norm_tiles)
    def _():
        o_ref[...] = (acc * cg_ref[...]).astype(o_ref.dtype)


def _in_proj(x2, gain, w, colgain, wf, *, norm_cols, head_dim):
    n, d = x2.shape
    c = w.shape[1]
    tm = _tile(n, 1024)
    tn = _tile(norm_cols, 512)
    assert c % tn == 0 and tn % head_dim == 0
    kern = functools.partial(_in_proj_kernel, norm_tiles=norm_cols // tn, head_dim=head_dim)
    return pl.pallas_call(
        kern,
        grid=(n // tm, c // tn),
        in_specs=[
            pl.BlockSpec((tm, d), lambda i, j: (i, 0)),
            pl.BlockSpec((1, d), lambda i, j: (0, 0)),
            pl.BlockSpec((d, tn), lambda i, j: (0, j)),
            pl.BlockSpec((1, tn), lambda i, j: (0, j)),
            pl.BlockSpec((d, LANES), lambda i, j: (0, 0)),
        ],
        out_specs=[
            pl.BlockSpec((tm, tn), lambda i, j: (i, j)),
            pl.BlockSpec((tm, LANES), lambda i, j: (i, 0)),
        ],
        out_shape=[
            jax.ShapeDtypeStruct((n, c), BF16),
            jax.ShapeDtypeStruct((n, LANES), F32),
        ],
        scratch_shapes=[pltpu.VMEM((tm, d), BF16)],
        compiler_params=_params("parallel", "arbitrary"),
        name="in_proj",
    )(x2, gain, w, colgain, wf)


def _forget_cum_kernel(f_ref, b_ref, c_ref, carry_ref):
    @pl.when(pl.program_id(1) == 0)
    def _():
        carry_ref[...] = jnp.zeros_like(carry_ref)

    z = f_ref[0] + b_ref[...]
    v = jnp.minimum(z, 0.0) - jnp.log1p(jnp.exp(-jnp.abs(z)))
    rows = v.shape[0]
    row = lax.broadcasted_iota(jnp.int32, v.shape, 0)
    shift = 1
    while shift < rows:
        v = v + jnp.where(row >= shift, pltpu.roll(v, shift, axis=0), 0.0)
        shift *= 2
    v = v + carry_ref[...]
    c_ref[0] = v
    carry_ref[...] = v[rows - 1:rows, :]


def _forget_cum(f3, bias):
    b, s, _ = f3.shape
    tc = _tile(s, 1024)
    return pl.pallas_call(
        _forget_cum_kernel,
        grid=(b, s // tc),
        in_specs=[
            pl.BlockSpec((1, tc, LANES), lambda i, t: (i, t, 0)),
            pl.BlockSpec((1, LANES), lambda i, t: (0, 0)),
        ],
        out_specs=pl.BlockSpec((1, tc, LANES), lambda i, t: (i, t, 0)),
        out_shape=jax.ShapeDtypeStruct(f3.shape, F32),
        scratch_shapes=[pltpu.VMEM((1, LANES), F32)],
        compiler_params=_params("parallel", "arbitrary"),
        name="forget_cum",
    )(f3, bias)


def _fox_kernel(q_ref, k_ref, v_ref, cq_ref, ck_ref, o_ref, *, blk):
    qi = pl.program_id(2)
    q = q_ref[0]
    cq = cq_ref[0]
    hd = q.shape[1]

    def step(kb, carry, masked):
        m, l, acc = carry
        start = pl.multiple_of(kb * blk, blk)
        k = k_ref[0, pl.ds(start, blk), :]
        v = v_ref[0, pl.ds(start, blk), :]
        ck = ck_ref[0, :, pl.ds(start, blk)]
        s = lax.dot_general(q, k, _NT, preferred_element_type=F32) + cq - ck
        if masked:
            row = lax.broadcasted_iota(jnp.int32, s.shape, 0)
            col = lax.broadcasted_iota(jnp.int32, s.shape, 1)
            s = jnp.where(col <= row, s, MASK_VALUE)
        m_new = jnp.maximum(m, jnp.max(s, axis=-1, keepdims=True))
        alpha = jnp.exp(m - m_new)
        p = jnp.exp(s - m_new)
        l = alpha * l + jnp.sum(p, axis=-1, keepdims=True)
        acc = alpha * acc + jnp.dot(p.astype(BF16), v, preferred_element_type=F32)
        return m_new, l, acc

    init = (jnp.full((blk, 1), MASK_VALUE, F32), jnp.zeros((blk, 1), F32), jnp.zeros((blk, hd), F32))
    carry = lax.fori_loop(0, qi, lambda kb, c: step(kb, c, False), init)
    _, l, acc = step(qi, carry, True)
    o_ref[0] = acc / l


def _fox_attn(proj3, c_col, c_row, *, n_heads, head_dim, q_off, k_off, v_off, out_width):
    b, s, _ = proj3.shape
    blk = _tile(s, 512)
    kern = functools.partial(_fox_kernel, blk=blk)
    return pl.pallas_call(
        kern,
        grid=(b, n_heads, s // blk),
        in_specs=[
            pl.BlockSpec((1, blk, head_dim), lambda bi, h, qi: (bi, qi, q_off + h)),
            pl.BlockSpec((1, s, head_dim), lambda bi, h, qi: (bi, 0, k_off + h)),
            pl.BlockSpec((1, s, head_dim), lambda bi, h, qi: (bi, 0, v_off + h)),
            pl.BlockSpec((1, blk, 1), lambda bi, h, qi: (bi * n_heads + h, qi, 0)),
            pl.BlockSpec((1, 1, s), lambda bi, h, qi: (bi * n_heads + h, 0, 0)),
        ],
        out_specs=pl.BlockSpec((1, blk, head_dim), lambda bi, h, qi: (bi, qi, h)),
        out_shape=jax.ShapeDtypeStruct((b, s, out_width), F32),
        compiler_params=_params("parallel", "parallel", "arbitrary"),
        name="fox_attn",
    )(proj3, proj3, proj3, c_col, c_row)


def _sb_kernel(q_ref, k_ref, v_ref, o_ref, *, blk, chunk):
    qi = pl.program_id(2)
    q = q_ref[0]
    hd = q.shape[1]
    n_chunks = blk // chunk
    tri = (lax.broadcasted_iota(jnp.int32, (chunk, chunk), 0)
           > lax.broadcasted_iota(jnp.int32, (chunk, chunk), 1)).astype(BF16)

    def step(kb, carry, masked):
        right, acc = carry
        start = pl.multiple_of(kb * blk, blk)
        k = k_ref[0, pl.ds(start, blk), :]
        z_all = lax.dot_general(q, k, _NT, preferred_element_type=F32)
        for c in reversed(range(n_chunks)):
            z = z_all[:, c * chunk:(c + 1) * chunk]
            v = v_ref[0, pl.ds(start + c * chunk, chunk), :]
            log_beta = jnp.minimum(z, 0.0) - jnp.log1p(jnp.exp(-jnp.abs(z)))
            log_om = log_beta - z
            if masked:
                row = lax.broadcasted_iota(jnp.int32, z.shape, 0)
                col = lax.broadcasted_iota(jnp.int32, z.shape, 1) + c * chunk
                allowed = col < row
                log_om = jnp.where(allowed, log_om, 0.0)
            hi = log_om.astype(BF16)
            lo = (log_om - hi.astype(F32)).astype(BF16)
            suffix = (jnp.dot(hi, tri, preferred_element_type=F32)
                      + jnp.dot(lo, tri, preferred_element_type=F32))
            w = jnp.exp(log_beta + suffix + right)
            if masked:
                w = jnp.where(allowed, w, 0.0)
            acc = acc + jnp.dot(w.astype(BF16), v, preferred_element_type=F32)
            right = right + jnp.sum(log_om, axis=-1, keepdims=True)
        return right, acc

    init = (jnp.zeros((blk, 1), F32), jnp.zeros((blk, hd), F32))
    carry = step(qi, init, True)
    _, acc = lax.fori_loop(0, qi, lambda t, c: step(qi - 1 - t, c, False), carry)
    o_ref[0] = acc


def _sb_attn(proj3, *, n_heads, head_dim, q_off, k_off, v_off, out_width):
    b, s, _ = proj3.shape
    blk = _tile(s, 512)
    chunk = _tile(blk, MXU_DIM)
    kern = functools.partial(_sb_kernel, blk=blk, chunk=chunk)
    return pl.pallas_call(
        kern,
        grid=(b, n_heads, s // blk),
        in_specs=[
            pl.BlockSpec((1, blk, head_dim), lambda bi, h, qi: (bi, qi, q_off + h)),
            pl.BlockSpec((1, s, head_dim), lambda bi, h, qi: (bi, 0, k_off + h)),
            pl.BlockSpec((1, s, head_dim), lambda bi, h, qi: (bi, 0, v_off + h)),
        ],
        out_specs=pl.BlockSpec((1, blk, head_dim), lambda bi, h, qi: (bi, qi, h)),
        out_shape=jax.ShapeDtypeStruct((b, s, out_width), F32),
        compiler_params=_params("parallel", "parallel", "arbitrary"),
        name="sb_attn",
    )(proj3, proj3, proj3)


def _out_proj_kernel(oa_ref, ob_ref, ga_ref, gb_ref, w_ref, x_ref, o_ref, m_ref):
    wa = oa_ref.shape[1]

    @pl.when(pl.program_id(1) == 0)
    def _():
        m_ref[:, :wa] = _rms(oa_ref[...], ga_ref[...]).astype(BF16)
        m_ref[:, wa:] = _rms(ob_ref[...], gb_ref[...]).astype(BF16)

    o_ref[...] = x_ref[...] + jnp.dot(m_ref[...], w_ref[...], preferred_element_type=F32)


def _out_proj(oa, ob, ga, gb, w, x2):
    n, d = x2.shape
    wa, wb = oa.shape[1], ob.shape[1]
    tm = _tile(n, 1024)
    tn = _tile(d, 1024)
    return pl.pallas_call(
        _out_proj_kernel,
        grid=(n // tm, d // tn),
        in_specs=[
            pl.BlockSpec((tm, wa), lambda i, j: (i, 0)),
            pl.BlockSpec((tm, wb), lambda i, j: (i, 0)),
            pl.BlockSpec((1, wa), lambda i, j: (0, 0)),
            pl.BlockSpec((1, wb), lambda i, j: (0, 0)),
            pl.BlockSpec((wa + wb, tn), lambda i, j: (0, j)),
            pl.BlockSpec((tm, tn), lambda i, j: (i, j)),
        ],
        out_specs=pl.BlockSpec((tm, tn), lambda i, j: (i, j)),
        out_shape=jax.ShapeDtypeStruct((n, d), F32),
        scratch_shapes=[pltpu.VMEM((tm, wa + wb), BF16)],
        compiler_params=_params("parallel", "arbitrary"),
        name="out_proj",
    )(oa, ob, ga, gb, w, x2)


def _conv_ffn_kernel(x_ref, xp_ref, g_ref, wg_ref, wv_ref, cwg_ref, cwv_ref, cbg_ref, cbv_ref, wd_ref,
                     o_ref, h_ref, *, tiles_per_seq):
    i = pl.program_id(0)
    tm = x_ref.shape[0]
    halo = xp_ref.shape[1]

    @pl.when(pl.program_id(1) == 0)
    def _():
        keep = (i % tiles_per_seq != 0).astype(F32)
        h_ref[:halo, :] = _rms(xp_ref[0] * keep, g_ref[...]).astype(BF16)
        h_ref[halo:, :] = _rms(x_ref[...], g_ref[...]).astype(BF16)
        o_ref[...] = x_ref[...]

    h = h_ref[...]

    def conv(w_ref, cw_ref, cb_ref):
        u = jnp.dot(h, w_ref[...], preferred_element_type=F32)
        cw = cw_ref[...]
        y = (cw[2:3, :] * u + cw[1:2, :] * pltpu.roll(u, 1, axis=0)
             + cw[0:1, :] * pltpu.roll(u, 2, axis=0) + cb_ref[...])
        return y[halo:, :]

    gate = conv(wg_ref, cwg_ref, cbg_ref)
    val = conv(wv_ref, cwv_ref, cbv_ref)
    act = (gate * (1.0 / (1.0 + jnp.exp(-gate))) * val).astype(BF16)
    o_ref[...] += jnp.dot(act, wd_ref[...], preferred_element_type=F32)


def _conv_ffn(x2, gain, w_up, conv_w, conv_b, w_down, *, seq):
    n, d = x2.shape
    f = w_down.shape[0]
    tm = _tile(seq, 512)
    tn = _tile(f, 512)
    halo = SUBLANES
    fb = f // tn
    xp = x2.reshape(n // halo, halo, d)
    kern = functools.partial(_conv_ffn_kernel, tiles_per_seq=seq // tm)
    return pl.pallas_call(
        kern,
        grid=(n // tm, fb),
        in_specs=[
            pl.BlockSpec((tm, d), lambda i, j: (i, 0)),
            pl.BlockSpec((1, halo, d), lambda i, j: (jnp.maximum(i * (tm // halo) - 1, 0), 0, 0)),
            pl.BlockSpec((1, d), lambda i, j: (0, 0)),
            pl.BlockSpec((d, tn), lambda i, j: (0, j)),
            pl.BlockSpec((d, tn), lambda i, j: (0, j + fb)),
            pl.BlockSpec((conv_w.shape[0], tn), lambda i, j: (0, j)),
            pl.BlockSpec((conv_w.shape[0], tn), lambda i, j: (0, j + fb)),
            pl.BlockSpec((1, tn), lambda i, j: (0, j)),
            pl.BlockSpec((1, tn), lambda i, j: (0, j + fb)),
            pl.BlockSpec((tn, d), lambda i, j: (j, 0)),
        ],
        out_specs=pl.BlockSpec((tm, d), lambda i, j: (i, 0)),
        out_shape=jax.ShapeDtypeStruct((n, d), F32),
        scratch_shapes=[pltpu.VMEM((halo + tm, d), BF16)],
        compiler_params=_params("parallel", "arbitrary"),
        name="conv_ffn",
    )(x2, xp, gain, w_up, w_up, conv_w, conv_w, conv_b, conv_b, w_down)


def kernel(x, attn_norm, w_in, b_forget, q_norm, k_norm, out_norm_fox, out_norm_sb, w_out, ffn_norm,
           w_up, conv_w, conv_b, w_down):
    batch, seq, d = x.shape
    depth = w_in.shape[0]
    hf = b_forget.shape[1]
    hd = q_norm.shape[1]
    wf_ = out_norm_fox.shape[1]
    ws_ = out_norm_sb.shape[1]
    hs = ws_ // hd
    n = batch * seq
    scale = hd ** -0.5
    assert w_in.shape[2] == 3 * wf_ + hf + 3 * ws_ and hf * hd == wf_ and hf <= LANES

    w_main = jnp.concatenate([w_in[:, :, :3 * wf_], w_in[:, :, 3 * wf_ + hf:]], axis=-1).astype(BF16)
    w_forget = jnp.pad(w_in[:, :, 3 * wf_:3 * wf_ + hf], ((0, 0), (0, 0), (0, LANES - hf))).astype(BF16)
    b_pad = jnp.pad(b_forget, ((0, 0), (0, LANES - hf)))
    ones_f = jnp.ones((depth, wf_), F32)
    ones_s = jnp.ones((depth, ws_), F32)
    colgain = jnp.concatenate([jnp.tile(q_norm * scale, (1, hf)), jnp.tile(k_norm, (1, hf)), ones_f,
                               ones_s * scale, ones_s, ones_s], axis=-1)
    w_out_b = w_out.astype(BF16)
    w_up_b = w_up.astype(BF16)
    w_down_b = w_down.astype(BF16)

    x2 = x.reshape(n, d)
    for l in range(depth):
        proj, f = _in_proj(x2, attn_norm[l][None], w_main[l], colgain[l][None], w_forget[l],
                           norm_cols=2 * wf_, head_dim=hd)
        c = _forget_cum(f.reshape(batch, seq, LANES), b_pad[l][None])
        c_heads = jnp.swapaxes(c[:, :, :hf], 1, 2).reshape(batch * hf, seq)
        proj3 = proj.reshape(batch, seq, proj.shape[1])
        out_a = _fox_attn(proj3, c_heads[:, :, None], c_heads[:, None, :], n_heads=hf, head_dim=hd,
                          q_off=0, k_off=hf, v_off=2 * hf, out_width=wf_)
        out_b = _sb_attn(proj3, n_heads=hs, head_dim=hd, q_off=3 * hf, k_off=3 * hf + hs,
                         v_off=3 * hf + 2 * hs, out_width=ws_)
        x2 = _out_proj(out_a.reshape(n, wf_), out_b.reshape(n, ws_), out_norm_fox[l][None],
                       out_norm_sb[l][None], w_out_b[l], x2)
        x2 = _conv_ffn(x2, ffn_norm[l][None], w_up_b[l], conv_w[l], conv_b[l][None], w_down_b[l], seq=seq)
    return x2.reshape(batch, seq, d)
```

```python
import functools

import jax
import jax.numpy as jnp
from jax import lax
from jax.experimental import pallas as pl
from jax.experimental.pallas import tpu as pltpu

EPS = 1e-6
LANES = 128
SUBLANES = 8
MXU_DIM = 256
VMEM_LIMIT_BYTES = 56 * 1024 * 1024
MASK_VALUE = -1e30
EXP_UNDERFLOW = 90.0
FOX_Q_BLOCK = 512
FOX_K_BLOCK = 512
FOX_HEADS_PER_STEP = 2
SB_HEADS_PER_STEP = 4

F32 = jnp.float32
BF16 = jnp.bfloat16
_NT = (((1,), (1,)), ((), ()))


def _tile(dim, want):
    t = min(dim, want)
    while dim % t:
        t -= 1
    return t


def _heads_per_step(want, n_heads, *col_offsets):
    g = want
    while g > 1 and any(v % g for v in (n_heads,) + col_offsets):
        g //= 2
    return g


def _run_skewed(chains):
    results = [None] * len(chains)
    started = 0
    live = []
    while started < len(chains) or live:
        if started < len(chains):
            live.append(started)
            started += 1
        for c in list(live):
            try:
                next(chains[c])
            except StopIteration as done:
                results[c] = done.value
                live.remove(c)
    return results


def _params(*sem):
    return pltpu.CompilerParams(dimension_semantics=sem, vmem_limit_bytes=VMEM_LIMIT_BYTES)


def _rms(x, gain):
    ms = jnp.mean(x * x, axis=-1, keepdims=True)
    return x * lax.rsqrt(ms + EPS) * gain


def _in_proj_kernel(x_ref, g_ref, w_ref, cg_ref, wf_ref, o_ref, f_ref, h_ref, *, norm_tiles, head_dim):
    j = pl.program_id(1)

    @pl.when(j == 0)
    def _():
        h = _rms(x_ref[...], g_ref[...]).astype(BF16)
        h_ref[...] = h
        f_ref[...] = jnp.dot(h, wf_ref[...], preferred_element_type=F32)

    acc = jnp.dot(h_ref[...], w_ref[...], preferred_element_type=F32)
    tn = acc.shape[1]

    @pl.when(j < norm_tiles)
    def _():
        for g in range(tn // head_dim):
            sl = slice(g * head_dim, (g + 1) * head_dim)
            o_ref[:, sl] = _rms(acc[:, sl], cg_ref[:, sl]).astype(o_ref.dtype)

    @pl.when(j >= norm_tiles)
    def _():
        o_ref[...] = (acc * cg_ref[...]).astype(o_ref.dtype)


def _in_proj(x2, gain, w, colgain, wf, *, norm_cols, head_dim):
    n, d = x2.shape
    c = w.shape[1]
    tm = _tile(n, 1024)
    tn = _tile(norm_cols, 512)
    assert c % tn == 0 and tn % head_dim == 0
    kern = functools.partial(_in_proj_kernel, norm_tiles=norm_cols // tn, head_dim=head_dim)
    return pl.pallas_call(
        kern,
        grid=(n // tm, c // tn),
        in_specs=[
            pl.BlockSpec((tm, d), lambda i, j: (i, 0)),
            pl.BlockSpec((1, d), lambda i, j: (0, 0)),
            pl.BlockSpec((d, tn), lambda i, j: (0, j)),
            pl.BlockSpec((1, tn), lambda i, j: (0, j)),
            pl.BlockSpec((d, LANES), lambda i, j: (0, 0)),
        ],
        out_specs=[
            pl.BlockSpec((tm, tn), lambda i, j: (i, j)),
            pl.BlockSpec((tm, LANES), lambda i, j: (i, 0)),
        ],
        out_shape=[
            jax.ShapeDtypeStruct((n, c), BF16),
            jax.ShapeDtypeStruct((n, LANES), F32),
        ],
        scratch_shapes=[pltpu.VMEM((tm, d), BF16)],
        compiler_params=_params("parallel", "arbitrary"),
        name="in_proj",
    )(x2, gain, w, colgain, wf)


def _forget_cum_kernel(f_ref, b_ref, c_ref, qa_ref, ka_ref, carry_ref, *, n_heads, head_dim):
    @pl.when(pl.program_id(1) == 0)
    def _():
        carry_ref[...] = jnp.zeros_like(carry_ref)

    z = f_ref[0] + b_ref[...]
    v = jnp.minimum(z, 0.0) - jnp.log1p(jnp.exp(-jnp.abs(z)))
    rows = v.shape[0]
    row = lax.broadcasted_iota(jnp.int32, v.shape, 0)
    shift = 1
    while shift < rows:
        v = v + jnp.where(row >= shift, pltpu.roll(v, shift, axis=0), 0.0)
        shift *= 2
    v = v + carry_ref[...]
    c_ref[0] = v
    carry_ref[...] = v[rows - 1:rows, :]

    lane = lax.broadcasted_iota(jnp.int32, (rows, head_dim), 1)
    for h in range(n_heads):
        c = jnp.broadcast_to(v[:, h:h + 1], (rows, head_dim))
        c1 = c.astype(BF16).astype(F32)
        c2 = (c - c1).astype(BF16).astype(F32)
        c3 = c - c1 - c2
        ones = jnp.where(lane < 6, 1.0, 0.0)
        q_cols = jnp.where(lane == 0, c1, jnp.where(lane == 1, c2, jnp.where(lane == 2, c3, ones)))
        k_cols = jnp.where(lane == 3, -c1, jnp.where(lane == 4, -c2, jnp.where(lane == 5, -c3, ones)))
        sl = slice(h * head_dim, (h + 1) * head_dim)
        qa_ref[0, :, sl] = q_cols.astype(BF16)
        ka_ref[0, :, sl] = k_cols.astype(BF16)


def _forget_cum(f3, bias, *, n_heads, head_dim):
    b, s, _ = f3.shape
    tc = _tile(s, 1024)
    wide = n_heads * head_dim
    kern = functools.partial(_forget_cum_kernel, n_heads=n_heads, head_dim=head_dim)
    return pl.pallas_call(
        kern,
        grid=(b, s // tc),
        in_specs=[
            pl.BlockSpec((1, tc, LANES), lambda i, t: (i, t, 0)),
            pl.BlockSpec((1, LANES), lambda i, t: (0, 0)),
        ],
        out_specs=[
            pl.BlockSpec((1, tc, LANES), lambda i, t: (i, t, 0)),
            pl.BlockSpec((1, tc, wide), lambda i, t: (i, t, 0)),
            pl.BlockSpec((1, tc, wide), lambda i, t: (i, t, 0)),
        ],
        out_shape=[
            jax.ShapeDtypeStruct(f3.shape, F32),
            jax.ShapeDtypeStruct((b, s, wide), BF16),
            jax.ShapeDtypeStruct((b, s, wide), BF16),
        ],
        scratch_shapes=[pltpu.VMEM((1, LANES), F32)],
        compiler_params=_params("parallel", "arbitrary"),
        name="forget_cum",
    )(f3, bias)


def _fox_kernel(cedge_ref, qk_bound_ref, q_ref, qa_ref, k_ref, ka_ref, v_ref, cq_ref, o_ref, *,
                blk, kblk, hd, heads):
    bi, g, qi = pl.program_id(0), pl.program_id(1), pl.program_id(2)
    first_head = (bi * pl.num_programs(1) + g) * heads
    cols = [slice(h * hd, (h + 1) * hd) for h in range(heads)]
    qs = [jnp.concatenate([q_ref[0, :, c], qa_ref[0, :, c]], axis=1) for c in cols]

    def blocks(h, kbs, state):
        m, l, acc = state
        for kb, diag in kbs:
            start = pl.multiple_of(kb * kblk, kblk)
            k = jnp.concatenate([k_ref[0, pl.ds(start, kblk), cols[h]], ka_ref[0, pl.ds(start, kblk), cols[h]]],
                                axis=1)
            s = lax.dot_general(qs[h], k, _NT, preferred_element_type=F32)
            yield
            if diag is not None:
                s = jnp.where(lax.broadcasted_iota(jnp.int32, s.shape, 1) + diag
                              <= lax.broadcasted_iota(jnp.int32, s.shape, 0), s, MASK_VALUE)
            m_new = jnp.maximum(m, jnp.max(s, axis=-1, keepdims=True))
            alpha = jnp.exp(m - m_new)
            p = jnp.exp(s - m_new)
            l = alpha * l + jnp.sum(p, axis=-1, keepdims=True)
            p_b = p.astype(BF16)
            m = m_new
            yield
            v = v_ref[0, pl.ds(start, kblk), cols[h]]
            acc = alpha * acc + jnp.dot(p_b, v, preferred_element_type=F32)
        return m, l, acc

    def more_left(kb, ms):
        slack = [jnp.max(cq_ref[h] - ms[h]) - cedge_ref[first_head + h, kb] for h in range(heads)]
        return functools.reduce(jnp.maximum, slack) + qk_bound_ref[0] > -EXP_UNDERFLOW

    per_q = blk // kblk
    first_kb = qi * per_q
    zeros = jnp.zeros((blk, 1), F32)
    init = (zeros + MASK_VALUE, zeros, jnp.zeros((blk, hd), F32))
    state = _run_skewed([blocks(h, [(first_kb + j, j * kblk) for j in range(per_q)], init) for h in range(heads)])
    ms, ls, accs = (tuple(s[i] for s in state) for i in range(3))

    def cond(c):
        t, go = c[0], c[1]
        return jnp.logical_and(t <= first_kb, go)

    def body(c):
        t, _, ms, ls, accs = c
        kb = first_kb - t
        state = _run_skewed([blocks(h, [(kb, None)], (ms[h], ls[h], accs[h])) for h in range(heads)])
        ms, ls, accs = (tuple(s[i] for s in state) for i in range(3))
        return t + 1, more_left(kb, ms), ms, ls, accs

    _, _, _, ls, accs = lax.while_loop(cond, body, (jnp.int32(1), more_left(first_kb, ms), ms, ls, accs))
    for h in range(heads):
        o_ref[0, :, cols[h]] = accs[h] / ls[h]


def _fox_attn(proj3, qa3, ka3, c_col, cedge, qk_bound, *, n_heads, head_dim, q_off, k_off, v_off, out_width):
    b, s, _ = proj3.shape
    kblk = s // cedge.shape[1]
    blk = _tile(s, FOX_Q_BLOCK)
    heads = _heads_per_step(FOX_HEADS_PER_STEP, n_heads, q_off, k_off, v_off)
    wide = heads * head_dim
    groups = n_heads // heads
    kern = functools.partial(_fox_kernel, blk=blk, kblk=kblk, hd=head_dim, heads=heads)
    grid_spec = pltpu.PrefetchScalarGridSpec(
        num_scalar_prefetch=2,
        grid=(b, groups, s // blk),
        in_specs=[
            pl.BlockSpec((1, blk, wide), lambda bi, g, qi, *_: (bi, qi, q_off // heads + g)),
            pl.BlockSpec((1, blk, wide), lambda bi, g, qi, *_: (bi, qi, g)),
            pl.BlockSpec((1, s, wide), lambda bi, g, qi, *_: (bi, 0, k_off // heads + g)),
            pl.BlockSpec((1, s, wide), lambda bi, g, qi, *_: (bi, 0, g)),
            pl.BlockSpec((1, s, wide), lambda bi, g, qi, *_: (bi, 0, v_off // heads + g)),
            pl.BlockSpec((heads, blk, 1), lambda bi, g, qi, *_: (bi * groups + g, qi, 0)),
        ],
        out_specs=pl.BlockSpec((1, blk, wide), lambda bi, g, qi, *_: (bi, qi, g)),
    )
    return pl.pallas_call(
        kern,
        grid_spec=grid_spec,
        out_shape=jax.ShapeDtypeStruct((b, s, out_width), F32),
        compiler_params=_params("parallel", "parallel", "arbitrary"),
        name="fox_attn",
    )(cedge, qk_bound, proj3, qa3, proj3, ka3, proj3, c_col)


def _sb_kernel(q_ref, k_ref, v_ref, o_ref, *, blk, hd, heads):
    qi = pl.program_id(2)
    tri = (lax.broadcasted_iota(jnp.int32, (blk, blk), 0)
           > lax.broadcasted_iota(jnp.int32, (blk, blk), 1)).astype(BF16)
    qs = [q_ref[0, :, h * hd:(h + 1) * hd] for h in range(heads)]

    def block(h, kb, masked):
        start = pl.multiple_of(kb * blk, blk)
        k = k_ref[0, pl.ds(start, blk), h * hd:(h + 1) * hd]
        z = lax.dot_general(qs[h], k, _NT, preferred_element_type=F32)
        yield
        sp = jnp.maximum(z, 0.0) + jnp.log(1.0 + jnp.exp(-jnp.abs(z)))
        log_beta = z - sp
        if masked:
            allowed = (lax.broadcasted_iota(jnp.int32, z.shape, 1)
                       < lax.broadcasted_iota(jnp.int32, z.shape, 0))
            sp = jnp.where(allowed, sp, 0.0)
        sp_sum = jnp.sum(sp, axis=-1, keepdims=True)
        sp_b = sp.astype(BF16)
        yield
        inner = jnp.dot(sp_b, tri, preferred_element_type=F32)
        yield
        w = jnp.exp(log_beta - inner)
        if masked:
            w = jnp.where(allowed, w, 0.0)
        w_b = w.astype(BF16)
        yield
        v = v_ref[0, pl.ds(start, blk), h * hd:(h + 1) * hd]
        return sp_sum, jnp.dot(w_b, v, preferred_element_type=F32)

    def alive(rights):
        return functools.reduce(jnp.maximum, [jnp.max(jnp.exp(-r)) for r in rights]) > 0.0

    has_left = (qi > 0).astype(F32)
    left = jnp.maximum(qi - 1, 0)
    first = _run_skewed([blk_ for h in range(heads) for blk_ in (block(h, qi, True), block(h, left, False))])
    rights, accs = [], []
    for h in range(heads):
        (sum_d, pv_d), (sum_l, pv_l) = first[2 * h], first[2 * h + 1]
        accs.append(pv_d + (has_left * jnp.exp(-sum_d)) * pv_l)
        rights.append(sum_d + sum_l)
    rights, accs = tuple(rights), tuple(accs)

    def cond(c):
        t, go, _, _ = c
        return jnp.logical_and(t < qi, go)

    def body(c):
        t, _, rights, accs = c
        new = _run_skewed([block(h, qi - 1 - t, False) for h in range(heads)])
        accs = tuple(accs[h] + jnp.exp(-rights[h]) * new[h][1] for h in range(heads))
        rights = tuple(rights[h] + new[h][0] for h in range(heads))
        return t + 1, alive(rights), rights, accs

    _, _, _, accs = lax.while_loop(cond, body, (jnp.int32(1), alive(rights), rights, accs))
    for h in range(heads):
        o_ref[0, :, h * hd:(h + 1) * hd] = accs[h]


def _sb_attn(proj3, *, n_heads, head_dim, q_off, k_off, v_off, out_width):
    b, s, _ = proj3.shape
    blk = _tile(s, MXU_DIM)
    heads = _heads_per_step(SB_HEADS_PER_STEP, n_heads, q_off, k_off, v_off)
    wide = heads * head_dim
    kern = functools.partial(_sb_kernel, blk=blk, hd=head_dim, heads=heads)
    return pl.pallas_call(
        kern,
        grid=(b, n_heads // heads, s // blk),
        in_specs=[
            pl.BlockSpec((1, blk, wide), lambda bi, g, qi: (bi, qi, q_off // heads + g)),
            pl.BlockSpec((1, s, wide), lambda bi, g, qi: (bi, 0, k_off // heads + g)),
            pl.BlockSpec((1, s, wide), lambda bi, g, qi: (bi, 0, v_off // heads + g)),
        ],
        out_specs=pl.BlockSpec((1, blk, wide), lambda bi, g, qi: (bi, qi, g)),
        out_shape=jax.ShapeDtypeStruct((b, s, out_width), F32),
        compiler_params=_params("parallel", "parallel", "arbitrary"),
        name="sb_attn",
    )(proj3, proj3, proj3)


def _out_proj_kernel(oa_ref, ob_ref, ga_ref, gb_ref, w_ref, x_ref, o_ref, m_ref):
    wa = oa_ref.shape[1]

    @pl.when(pl.program_id(1) == 0)
    def _():
        m_ref[:, :wa] = _rms(oa_ref[...], ga_ref[...]).astype(BF16)
        m_ref[:, wa:] = _rms(ob_ref[...], gb_ref[...]).astype(BF16)

    o_ref[...] = x_ref[...] + jnp.dot(m_ref[...], w_ref[...], preferred_element_type=F32)


def _out_proj(oa, ob, ga, gb, w, x2):
    n, d = x2.shape
    wa, wb = oa.shape[1], ob.shape[1]
    tm = _tile(n, 1024)
    tn = _tile(d, 1024)
    return pl.pallas_call(
        _out_proj_kernel,
        grid=(n // tm, d // tn),
        in_specs=[
            pl.BlockSpec((tm, wa), lambda i, j: (i, 0)),
            pl.BlockSpec((tm, wb), lambda i, j: (i, 0)),
            pl.BlockSpec((1, wa), lambda i, j: (0, 0)),
            pl.BlockSpec((1, wb), lambda i, j: (0, 0)),
            pl.BlockSpec((wa + wb, tn), lambda i, j: (0, j)),
            pl.BlockSpec((tm, tn), lambda i, j: (i, j)),
        ],
        out_specs=pl.BlockSpec((tm, tn), lambda i, j: (i, j)),
        out_shape=jax.ShapeDtypeStruct((n, d), F32),
        scratch_shapes=[pltpu.VMEM((tm, wa + wb), BF16)],
        compiler_params=_params("parallel", "arbitrary"),
        name="out_proj",
    )(oa, ob, ga, gb, w, x2)


def _conv_ffn_kernel(x_ref, xp_ref, g_ref, wg_ref, wv_ref, cwg_ref, cwv_ref, cbg_ref, cbv_ref, wd_ref,
                     o_ref, h_ref, *, tiles_per_seq):
    i = pl.program_id(0)
    tm = x_ref.shape[0]
    halo = xp_ref.shape[1]

    @pl.when(pl.program_id(1) == 0)
    def _():
        keep = (i % tiles_per_seq != 0).astype(F32)
        h_ref[:halo, :] = _rms(xp_ref[0] * keep, g_ref[...]).astype(BF16)
        h_ref[halo:, :] = _rms(x_ref[...], g_ref[...]).astype(BF16)
        o_ref[...] = x_ref[...]

    h = h_ref[...]

    def conv(w_ref, cw_ref, cb_ref):
        u = jnp.dot(h, w_ref[...], preferred_element_type=F32)
        cw = cw_ref[...]
        y = (cw[2:3, :] * u + cw[1:2, :] * pltpu.roll(u, 1, axis=0)
             + cw[0:1, :] * pltpu.roll(u, 2, axis=0) + cb_ref[...])
        return y[halo:, :]

    gate = conv(wg_ref, cwg_ref, cbg_ref)
    val = conv(wv_ref, cwv_ref, cbv_ref)
    act = (gate * (1.0 / (1.0 + jnp.exp(-gate))) * val).astype(BF16)
    o_ref[...] += jnp.dot(act, wd_ref[...], preferred_element_type=F32)


def _conv_ffn(x2, gain, w_up, conv_w, conv_b, w_down, *, seq):
    n, d = x2.shape
    f = w_down.shape[0]
    tm = _tile(seq, 512)
    tn = _tile(f, 512)
    halo = SUBLANES
    fb = f // tn
    xp = x2.reshape(n // halo, halo, d)
    kern = functools.partial(_conv_ffn_kernel, tiles_per_seq=seq // tm)
    return pl.pallas_call(
        kern,
        grid=(n // tm, fb),
        in_specs=[
            pl.BlockSpec((tm, d), lambda i, j: (i, 0)),
            pl.BlockSpec((1, halo, d), lambda i, j: (jnp.maximum(i * (tm // halo) - 1, 0), 0, 0)),
            pl.BlockSpec((1, d), lambda i, j: (0, 0)),
            pl.BlockSpec((d, tn), lambda i, j: (0, j)),
            pl.BlockSpec((d, tn), lambda i, j: (0, j + fb)),
            pl.BlockSpec((conv_w.shape[0], tn), lambda i, j: (0, j)),
            pl.BlockSpec((conv_w.shape[0], tn), lambda i, j: (0, j + fb)),
            pl.BlockSpec((1, tn), lambda i, j: (0, j)),
            pl.BlockSpec((1, tn), lambda i, j: (0, j + fb)),
            pl.BlockSpec((tn, d), lambda i, j: (j, 0)),
        ],
        out_specs=pl.BlockSpec((tm, d), lambda i, j: (i, 0)),
        out_shape=jax.ShapeDtypeStruct((n, d), F32),
        scratch_shapes=[pltpu.VMEM((halo + tm, d), BF16)],
        compiler_params=_params("parallel", "arbitrary"),
        name="conv_ffn",
    )(x2, xp, gain, w_up, w_up, conv_w, conv_w, conv_b, conv_b, w_down)


def kernel(x, attn_norm, w_in, b_forget, q_norm, k_norm, out_norm_fox, out_norm_sb, w_out, ffn_norm,
           w_up, conv_w, conv_b, w_down):
    batch, seq, d = x.shape
    depth = w_in.shape[0]
    hf = b_forget.shape[1]
    hd = q_norm.shape[1]
    wf_ = out_norm_fox.shape[1]
    ws_ = out_norm_sb.shape[1]
    hs = ws_ // hd
    n = batch * seq
    scale = hd ** -0.5
    fox_blk = _tile(seq, FOX_K_BLOCK)
    assert w_in.shape[2] == 3 * wf_ + hf + 3 * ws_ and hf * hd == wf_ and hf <= LANES

    w_main = jnp.concatenate([w_in[:, :, :3 * wf_], w_in[:, :, 3 * wf_ + hf:]], axis=-1).astype(BF16)
    w_forget = jnp.pad(w_in[:, :, 3 * wf_:3 * wf_ + hf], ((0, 0), (0, 0), (0, LANES - hf))).astype(BF16)
    b_pad = jnp.pad(b_forget, ((0, 0), (0, LANES - hf)))
    ones_f = jnp.ones((depth, wf_), F32)
    ones_s = jnp.ones((depth, ws_), F32)
    colgain = jnp.concatenate([jnp.tile(q_norm * scale, (1, hf)), jnp.tile(k_norm, (1, hf)), ones_f,
                               ones_s * scale, ones_s, ones_s], axis=-1)
    w_out_b = w_out.astype(BF16)
    w_up_b = w_up.astype(BF16)
    w_down_b = w_down.astype(BF16)

    x2 = x.reshape(n, d)
    for l in range(depth):
        proj, f = _in_proj(x2, attn_norm[l][None], w_main[l], colgain[l][None], w_forget[l],
                           norm_cols=2 * wf_, head_dim=hd)
        c, qa, ka = _forget_cum(f.reshape(batch, seq, LANES), b_pad[l][None], n_heads=hf, head_dim=hd)
        c_heads = jnp.swapaxes(c[:, :, :hf], 1, 2).reshape(batch * hf, seq)
        c_edge = c_heads[:, ::fox_blk]
        qk_bound = (1.01 * hd * scale * jnp.max(jnp.abs(q_norm[l])) * jnp.max(jnp.abs(k_norm[l]))).reshape(1)
        proj3 = proj.reshape(batch, seq, proj.shape[1])
        out_a = _fox_attn(proj3, qa, ka, c_heads[:, :, None], c_edge, qk_bound, n_heads=hf, head_dim=hd,
                          q_off=0, k_off=hf, v_off=2 * hf, out_width=wf_)
        out_b = _sb_attn(proj3, n_heads=hs, head_dim=hd, q_off=3 * hf, k_off=3 * hf + hs,
                         v_off=3 * hf + 2 * hs, out_width=ws_)
        x2 = _out_proj(out_a.reshape(n, wf_), out_b.reshape(n, ws_), out_norm_fox[l][None],
                       out_norm_sb[l][None], w_out_b[l], x2)
        x2 = _conv_ffn(x2, ffn_norm[l][None], w_up_b[l], conv_w[l], conv_b[l][None], w_down_b[l], seq=seq)
    return x2.reshape(batch, seq, d)
```

```python
import functools
import math

import jax
import jax.numpy as jnp
from jax import lax
from jax.experimental import pallas as pl
from jax.experimental.pallas import tpu as pltpu

EPS = 1e-6
LANES = 128
SUBLANES = 8
MXU_DIM = 256
VMEM_LIMIT_BYTES = 56 * 1024 * 1024
MASK_VALUE = -1e30
EXP_UNDERFLOW = 90.0
FOX_Q_BLOCK = 512
FOX_K_BLOCK = 512
FOX_HEADS_PER_STEP = 2
SB_HEADS_PER_STEP = 4
FFN_ROW_CHAINS = 2
OUT_ROW_CHAINS = 2
IN_ROW_CHAINS = 2

F32 = jnp.float32
BF16 = jnp.bfloat16
_NT = (((1,), (1,)), ((), ()))


def _tile(dim, want):
    t = min(dim, want)
    while dim % t:
        t -= 1
    return t


def _heads_per_step(want, n_heads, *col_offsets):
    g = want
    while g > 1 and any(v % g for v in (n_heads,) + col_offsets):
        g //= 2
    return g


def _run_skewed(chains):
    results = [None] * len(chains)
    started = 0
    live = []
    while started < len(chains) or live:
        if started < len(chains):
            live.append(started)
            started += 1
        for c in list(live):
            try:
                next(chains[c])
            except StopIteration as done:
                results[c] = done.value
                live.remove(c)
    return results


def _params(*sem):
    return pltpu.CompilerParams(dimension_semantics=sem, vmem_limit_bytes=VMEM_LIMIT_BYTES)


def _rms(x, gain):
    ms = jnp.mean(x * x, axis=-1, keepdims=True)
    return x * lax.rsqrt(ms + EPS) * gain


def _in_proj_kernel(x_ref, g_ref, wa_ref, wb_ref, cg_ref, wf_ref, o_ref, f_ref, h_ref, *,
                    norm_tiles, a_tiles, head_dim, row_chains):
    j = pl.program_id(1)
    tm, tn = o_ref.shape

    @pl.when(j == 0)
    def _():
        h = _rms(x_ref[...], g_ref[...]).astype(BF16)
        h_ref[...] = h
        f_ref[...] = jnp.dot(h, wf_ref[...], preferred_element_type=F32)

    def rows(w_ref, normed, r0, nrows):
        rsl = slice(r0, r0 + nrows)
        acc = jnp.dot(h_ref[rsl, :], w_ref[...], preferred_element_type=F32)
        yield
        if normed:
            for g in range(tn // head_dim):
                sl = slice(g * head_dim, (g + 1) * head_dim)
                o_ref[rsl, sl] = _rms(acc[:, sl], cg_ref[:, sl]).astype(o_ref.dtype)
        else:
            o_ref[rsl, :] = (acc * cg_ref[...]).astype(o_ref.dtype)

    def tile(w_ref, normed):
        nrows = tm // row_chains
        _run_skewed([rows(w_ref, normed, c * nrows, nrows) for c in range(row_chains)])

    pl.when(j < norm_tiles)(lambda: tile(wa_ref, True))
    pl.when(jnp.logical_and(j >= norm_tiles, j < a_tiles))(lambda: tile(wa_ref, False))
    pl.when(j >= a_tiles)(lambda: tile(wb_ref, False))


def _in_proj(x2, gain, wa, wb, colgain, wf, *, norm_cols, head_dim):
    n, d = x2.shape
    ca, cb = wa.shape[1], wb.shape[1]
    tm = _tile(n, 1024)
    tn = _tile(math.gcd(norm_cols, ca, cb), 512)
    assert tn % head_dim == 0
    a_tiles = ca // tn
    row_chains = IN_ROW_CHAINS if tm % (IN_ROW_CHAINS * 2 * SUBLANES) == 0 else 1
    kern = functools.partial(_in_proj_kernel, norm_tiles=norm_cols // tn, a_tiles=a_tiles, head_dim=head_dim,
                             row_chains=row_chains)
    return pl.pallas_call(
        kern,
        grid=(n // tm, (ca + cb) // tn),
        in_specs=[
            pl.BlockSpec((tm, d), lambda i, j: (i, 0)),
            pl.BlockSpec((1, d), lambda i, j: (0, 0)),
            pl.BlockSpec((d, tn), lambda i, j: (0, jnp.minimum(j, a_tiles - 1))),
            pl.BlockSpec((d, tn), lambda i, j: (0, jnp.maximum(j - a_tiles, 0))),
            pl.BlockSpec((1, tn), lambda i, j: (0, j)),
            pl.BlockSpec((d, LANES), lambda i, j: (0, 0)),
        ],
        out_specs=[
            pl.BlockSpec((tm, tn), lambda i, j: (i, j)),
            pl.BlockSpec((tm, LANES), lambda i, j: (i, 0)),
        ],
        out_shape=[
            jax.ShapeDtypeStruct((n, ca + cb), BF16),
            jax.ShapeDtypeStruct((n, LANES), F32),
        ],
        scratch_shapes=[pltpu.VMEM((tm, d), BF16)],
        compiler_params=_params("parallel", "arbitrary"),
        name="in_proj",
    )(x2, gain, wa, wb, colgain, wf)


def _forget_cum_kernel(f_ref, b_ref, c_ref, qa_ref, ka_ref, carry_ref, *, n_heads, head_dim):
    @pl.when(pl.program_id(1) == 0)
    def _():
        carry_ref[...] = jnp.zeros_like(carry_ref)

    z = f_ref[0] + b_ref[...]
    v = jnp.minimum(z, 0.0) - jnp.log1p(jnp.exp(-jnp.abs(z)))
    rows = v.shape[0]
    row = lax.broadcasted_iota(jnp.int32, v.shape, 0)
    shift = 1
    while shift < rows:
        v = v + jnp.where(row >= shift, pltpu.roll(v, shift, axis=0), 0.0)
        shift *= 2
    v = v + carry_ref[...]
    c_ref[0] = v
    carry_ref[...] = v[rows - 1:rows, :]

    lane = lax.broadcasted_iota(jnp.int32, (rows, head_dim), 1)
    for h in range(n_heads):
        c = jnp.broadcast_to(v[:, h:h + 1], (rows, head_dim))
        c1 = c.astype(BF16).astype(F32)
        c2 = (c - c1).astype(BF16).astype(F32)
        c3 = c - c1 - c2
        ones = jnp.where(lane < 6, 1.0, 0.0)
        q_cols = jnp.where(lane == 0, c1, jnp.where(lane == 1, c2, jnp.where(lane == 2, c3, ones)))
        k_cols = jnp.where(lane == 3, -c1, jnp.where(lane == 4, -c2, jnp.where(lane == 5, -c3, ones)))
        sl = slice(h * head_dim, (h + 1) * head_dim)
        qa_ref[0, :, sl] = q_cols.astype(BF16)
        ka_ref[0, :, sl] = k_cols.astype(BF16)


def _forget_cum(f3, bias, *, n_heads, head_dim):
    b, s, _ = f3.shape
    tc = _tile(s, 1024)
    wide = n_heads * head_dim
    kern = functools.partial(_forget_cum_kernel, n_heads=n_heads, head_dim=head_dim)
    return pl.pallas_call(
        kern,
        grid=(b, s // tc),
        in_specs=[
            pl.BlockSpec((1, tc, LANES), lambda i, t: (i, t, 0)),
            pl.BlockSpec((1, LANES), lambda i, t: (0, 0)),
        ],
        out_specs=[
            pl.BlockSpec((1, tc, LANES), lambda i, t: (i, t, 0)),
            pl.BlockSpec((1, tc, wide), lambda i, t: (i, t, 0)),
            pl.BlockSpec((1, tc, wide), lambda i, t: (i, t, 0)),
        ],
        out_shape=[
            jax.ShapeDtypeStruct(f3.shape, F32),
            jax.ShapeDtypeStruct((b, s, wide), BF16),
            jax.ShapeDtypeStruct((b, s, wide), BF16),
        ],
        scratch_shapes=[pltpu.VMEM((1, LANES), F32)],
        compiler_params=_params("parallel", "arbitrary"),
        name="forget_cum",
    )(f3, bias)


def _fox_kernel(cedge_ref, qk_bound_ref, q_ref, qa_ref, k_ref, ka_ref, v_ref, cq_ref, o_ref, *,
                blk, kblk, hd, heads):
    bi, g, qi = pl.program_id(0), pl.program_id(1), pl.program_id(2)
    first_head = (bi * pl.num_programs(1) + g) * heads
    cols = [slice(h * hd, (h + 1) * hd) for h in range(heads)]
    qs = [jnp.concatenate([q_ref[0, :, c], qa_ref[0, :, c]], axis=1) for c in cols]

    def blocks(h, kbs, state):
        m, l, acc = state
        for kb, diag in kbs:
            start = pl.multiple_of(kb * kblk, kblk)
            k = jnp.concatenate([k_ref[0, pl.ds(start, kblk), cols[h]], ka_ref[0, pl.ds(start, kblk), cols[h]]],
                                axis=1)
            s = lax.dot_general(qs[h], k, _NT, preferred_element_type=F32)
            yield
            if diag is not None:
                s = jnp.where(lax.broadcasted_iota(jnp.int32, s.shape, 1) + diag
                              <= lax.broadcasted_iota(jnp.int32, s.shape, 0), s, MASK_VALUE)
            m_new = jnp.maximum(m, jnp.max(s, axis=-1, keepdims=True))
            alpha = jnp.exp(m - m_new)
            p = jnp.exp(s - m_new)
            l = alpha * l + jnp.sum(p, axis=-1, keepdims=True)
            p_b = p.astype(BF16)
            m = m_new
            yield
            v = v_ref[0, pl.ds(start, kblk), cols[h]]
            acc = alpha * acc + jnp.dot(p_b, v, preferred_element_type=F32)
        return m, l, acc

    def more_left(kb, ms):
        slack = [jnp.max(cq_ref[h] - ms[h]) - cedge_ref[first_head + h, kb] for h in range(heads)]
        return functools.reduce(jnp.maximum, slack) + qk_bound_ref[0] > -EXP_UNDERFLOW

    per_q = blk // kblk
    first_kb = qi * per_q
    zeros = jnp.zeros((blk, 1), F32)
    init = (zeros + MASK_VALUE, zeros, jnp.zeros((blk, hd), F32))
    state = _run_skewed([blocks(h, [(first_kb + j, j * kblk) for j in range(per_q)], init) for h in range(heads)])
    ms, ls, accs = (tuple(s[i] for s in state) for i in range(3))

    def cond(c):
        t, go = c[0], c[1]
        return jnp.logical_and(t <= first_kb, go)

    def body(c):
        t, _, ms, ls, accs = c
        kb = first_kb - t
        state = _run_skewed([blocks(h, [(kb, None)], (ms[h], ls[h], accs[h])) for h in range(heads)])
        ms, ls, accs = (tuple(s[i] for s in state) for i in range(3))
        return t + 1, more_left(kb, ms), ms, ls, accs

    _, _, _, ls, accs = lax.while_loop(cond, body, (jnp.int32(1), more_left(first_kb, ms), ms, ls, accs))
    for h in range(heads):
        o_ref[0, :, cols[h]] = accs[h] / ls[h]


def _fox_attn(proj3, qa3, ka3, c_col, cedge, qk_bound, *, n_heads, head_dim, q_off, k_off, v_off, out_width):
    b, s, _ = proj3.shape
    kblk = s // cedge.shape[1]
    blk = _tile(s, FOX_Q_BLOCK)
    heads = _heads_per_step(FOX_HEADS_PER_STEP, n_heads, q_off, k_off, v_off)
    wide = heads * head_dim
    groups = n_heads // heads
    kern = functools.partial(_fox_kernel, blk=blk, kblk=kblk, hd=head_dim, heads=heads)
    grid_spec = pltpu.PrefetchScalarGridSpec(
        num_scalar_prefetch=2,
        grid=(b, groups, s // blk),
        in_specs=[
            pl.BlockSpec((1, blk, wide), lambda bi, g, qi, *_: (bi, qi, q_off // heads + g)),
            pl.BlockSpec((1, blk, wide), lambda bi, g, qi, *_: (bi, qi, g)),
            pl.BlockSpec((1, s, wide), lambda bi, g, qi, *_: (bi, 0, k_off // heads + g)),
            pl.BlockSpec((1, s, wide), lambda bi, g, qi, *_: (bi, 0, g)),
            pl.BlockSpec((1, s, wide), lambda bi, g, qi, *_: (bi, 0, v_off // heads + g)),
            pl.BlockSpec((heads, blk, 1), lambda bi, g, qi, *_: (bi * groups + g, qi, 0)),
        ],
        out_specs=pl.BlockSpec((1, blk, wide), lambda bi, g, qi, *_: (bi, qi, g)),
    )
    return pl.pallas_call(
        kern,
        grid_spec=grid_spec,
        out_shape=jax.ShapeDtypeStruct((b, s, out_width), F32),
        compiler_params=_params("parallel", "parallel", "arbitrary"),
        name="fox_attn",
    )(cedge, qk_bound, proj3, qa3, proj3, ka3, proj3, c_col)


def _sb_kernel(q_ref, k_ref, v_ref, o_ref, *, blk, hd, heads):
    qi = pl.program_id(2)
    tri = (lax.broadcasted_iota(jnp.int32, (blk, blk), 0)
           > lax.broadcasted_iota(jnp.int32, (blk, blk), 1)).astype(BF16)
    qs = [q_ref[0, :, h * hd:(h + 1) * hd] for h in range(heads)]

    def block(h, kb, masked):
        start = pl.multiple_of(kb * blk, blk)
        k = k_ref[0, pl.ds(start, blk), h * hd:(h + 1) * hd]
        z = lax.dot_general(qs[h], k, _NT, preferred_element_type=F32)
        yield
        sp = jnp.maximum(z, 0.0) + jnp.log(1.0 + jnp.exp(-jnp.abs(z)))
        log_beta = z - sp
        if masked:
            allowed = (lax.broadcasted_iota(jnp.int32, z.shape, 1)
                       < lax.broadcasted_iota(jnp.int32, z.shape, 0))
            sp = jnp.where(allowed, sp, 0.0)
        sp_sum = jnp.sum(sp, axis=-1, keepdims=True)
        sp_b = sp.astype(BF16)
        yield
        inner = jnp.dot(sp_b, tri, preferred_element_type=F32)
        yield
        w = jnp.exp(log_beta - inner)
        if masked:
            w = jnp.where(allowed, w, 0.0)
        w_b = w.astype(BF16)
        yield
        v = v_ref[0, pl.ds(start, blk), h * hd:(h + 1) * hd]
        return sp_sum, jnp.dot(w_b, v, preferred_element_type=F32)

    def alive(rights):
        return functools.reduce(jnp.maximum, [jnp.max(jnp.exp(-r)) for r in rights]) > 0.0

    has_left = (qi > 0).astype(F32)
    left = jnp.maximum(qi - 1, 0)
    first = _run_skewed([blk_ for h in range(heads) for blk_ in (block(h, qi, True), block(h, left, False))])
    rights, accs = [], []
    for h in range(heads):
        (sum_d, pv_d), (sum_l, pv_l) = first[2 * h], first[2 * h + 1]
        accs.append(pv_d + (has_left * jnp.exp(-sum_d)) * pv_l)
        rights.append(sum_d + sum_l)
    rights, accs = tuple(rights), tuple(accs)

    def cond(c):
        t, go, _, _ = c
        return jnp.logical_and(t < qi, go)

    def body(c):
        t, _, rights, accs = c
        new = _run_skewed([block(h, qi - 1 - t, False) for h in range(heads)])
        accs = tuple(accs[h] + jnp.exp(-rights[h]) * new[h][1] for h in range(heads))
        rights = tuple(rights[h] + new[h][0] for h in range(heads))
        return t + 1, alive(rights), rights, accs

    _, _, _, accs = lax.while_loop(cond, body, (jnp.int32(1), alive(rights), rights, accs))
    for h in range(heads):
        o_ref[0, :, h * hd:(h + 1) * hd] = accs[h]


def _sb_attn(proj3, *, n_heads, head_dim, q_off, k_off, v_off, out_width):
    b, s, _ = proj3.shape
    blk = _tile(s, MXU_DIM)
    heads = _heads_per_step(SB_HEADS_PER_STEP, n_heads, q_off, k_off, v_off)
    wide = heads * head_dim
    kern = functools.partial(_sb_kernel, blk=blk, hd=head_dim, heads=heads)
    return pl.pallas_call(
        kern,
        grid=(b, n_heads // heads, s // blk),
        in_specs=[
            pl.BlockSpec((1, blk, wide), lambda bi, g, qi: (bi, qi, q_off // heads + g)),
            pl.BlockSpec((1, s, wide), lambda bi, g, qi: (bi, 0, k_off // heads + g)),
            pl.BlockSpec((1, s, wide), lambda bi, g, qi: (bi, 0, v_off // heads + g)),
        ],
        out_specs=pl.BlockSpec((1, blk, wide), lambda bi, g, qi: (bi, qi, g)),
        out_shape=jax.ShapeDtypeStruct((b, s, out_width), F32),
        compiler_params=_params("parallel", "parallel", "arbitrary"),
        name="sb_attn",
    )(proj3, proj3, proj3)


def _out_proj_kernel(oa_ref, ob_ref, ga_ref, gb_ref, w_ref, x_ref, o_ref, *, row_chains):
    def rows(r0, nrows):
        sl = slice(r0, r0 + nrows)
        merged = jnp.concatenate([_rms(oa_ref[sl, :], ga_ref[...]).astype(BF16),
                                  _rms(ob_ref[sl, :], gb_ref[...]).astype(BF16)], axis=1)
        yield
        o_ref[sl, :] = x_ref[sl, :] + jnp.dot(merged, w_ref[...], preferred_element_type=F32)

    nrows = o_ref.shape[0] // row_chains
    _run_skewed([rows(c * nrows, nrows) for c in range(row_chains)])


def _out_proj(oa, ob, ga, gb, w, x2):
    n, d = x2.shape
    wa, wb = oa.shape[1], ob.shape[1]
    tm = _tile(n, 512)
    row_chains = OUT_ROW_CHAINS if tm % (OUT_ROW_CHAINS * SUBLANES) == 0 else 1
    return pl.pallas_call(
        functools.partial(_out_proj_kernel, row_chains=row_chains),
        grid=(n // tm,),
        in_specs=[
            pl.BlockSpec((tm, wa), lambda i: (i, 0)),
            pl.BlockSpec((tm, wb), lambda i: (i, 0)),
            pl.BlockSpec((1, wa), lambda i: (0, 0)),
            pl.BlockSpec((1, wb), lambda i: (0, 0)),
            pl.BlockSpec((wa + wb, d), lambda i: (0, 0)),
            pl.BlockSpec((tm, d), lambda i: (i, 0)),
        ],
        out_specs=pl.BlockSpec((tm, d), lambda i: (i, 0)),
        out_shape=jax.ShapeDtypeStruct((n, d), F32),
        compiler_params=_params("parallel"),
        name="out_proj",
    )(oa, ob, ga, gb, w, x2)


def _conv_ffn_kernel(x_ref, xp_ref, g_ref, wg_ref, wv_ref, cwg_ref, cwv_ref, cbg_ref, cbv_ref, wd_ref,
                     o_ref, h_ref, ug_ref, uv_ref, *, tiles_per_seq, row_chains):
    i = pl.program_id(0)
    tm = x_ref.shape[0]
    halo = xp_ref.shape[1]

    @pl.when(pl.program_id(1) == 0)
    def _():
        keep = (i % tiles_per_seq != 0).astype(F32)
        h_ref[:halo, :] = _rms(xp_ref[0] * keep, g_ref[...]).astype(BF16)
        h_ref[halo:, :] = _rms(x_ref[...], g_ref[...]).astype(BF16)
        o_ref[...] = x_ref[...]

    def conv(u_ref, nrows, cw_ref, cb_ref):
        cw = cw_ref[...]
        return (cw[2:3, :] * u_ref[halo:halo + nrows, :] + cw[1:2, :] * u_ref[halo - 1:halo - 1 + nrows, :]
                + cw[0:1, :] * u_ref[halo - 2:halo - 2 + nrows, :] + cb_ref[...])

    def rows(c, r0, nrows):
        h = h_ref[r0:r0 + halo + nrows, :]
        ug_ref[c] = jnp.dot(h, wg_ref[...], preferred_element_type=F32)
        uv_ref[c] = jnp.dot(h, wv_ref[...], preferred_element_type=F32)
        yield
        gate = conv(ug_ref.at[c], nrows, cwg_ref, cbg_ref)
        val = conv(uv_ref.at[c], nrows, cwv_ref, cbv_ref)
        act = (gate * (1.0 / (1.0 + jnp.exp(-gate))) * val).astype(BF16)
        yield
        o_ref[r0:r0 + nrows, :] += jnp.dot(act, wd_ref[...], preferred_element_type=F32)

    nrows = tm // row_chains
    _run_skewed([rows(c, c * nrows, nrows) for c in range(row_chains)])


def _conv_ffn(x2, gain, w_up, conv_w, conv_b, w_down, *, seq):
    n, d = x2.shape
    f = w_down.shape[0]
    tm = _tile(seq, 512)
    tn = _tile(f, 512)
    halo = SUBLANES
    fb = f // tn
    xp = x2.reshape(n // halo, halo, d)
    row_chains = FFN_ROW_CHAINS if tm % (FFN_ROW_CHAINS * SUBLANES) == 0 else 1
    kern = functools.partial(_conv_ffn_kernel, tiles_per_seq=seq // tm, row_chains=row_chains)
    return pl.pallas_call(
        kern,
        grid=(n // tm, fb),
        in_specs=[
            pl.BlockSpec((tm, d), lambda i, j: (i, 0)),
            pl.BlockSpec((1, halo, d), lambda i, j: (jnp.maximum(i * (tm // halo) - 1, 0), 0, 0)),
            pl.BlockSpec((1, d), lambda i, j: (0, 0)),
            pl.BlockSpec((d, tn), lambda i, j: (0, j)),
            pl.BlockSpec((d, tn), lambda i, j: (0, j + fb)),
            pl.BlockSpec((conv_w.shape[0], tn), lambda i, j: (0, j)),
            pl.BlockSpec((conv_w.shape[0], tn), lambda i, j: (0, j + fb)),
            pl.BlockSpec((1, tn), lambda i, j: (0, j)),
            pl.BlockSpec((1, tn), lambda i, j: (0, j + fb)),
            pl.BlockSpec((tn, d), lambda i, j: (j, 0)),
        ],
        out_specs=pl.BlockSpec((tm, d), lambda i, j: (i, 0)),
        out_shape=jax.ShapeDtypeStruct((n, d), F32),
        scratch_shapes=[pltpu.VMEM((halo + tm, d), BF16),
                        pltpu.VMEM((row_chains, halo + tm // row_chains, tn), F32),
                        pltpu.VMEM((row_chains, halo + tm // row_chains, tn), F32)],
        compiler_params=_params("parallel", "arbitrary"),
        name="conv_ffn",
    )(x2, xp, gain, w_up, w_up, conv_w, conv_w, conv_b, conv_b, w_down)


def kernel(x, attn_norm, w_in, b_forget, q_norm, k_norm, out_norm_fox, out_norm_sb, w_out, ffn_norm,
           w_up, conv_w, conv_b, w_down):
    batch, seq, d = x.shape
    depth = w_in.shape[0]
    hf = b_forget.shape[1]
    hd = q_norm.shape[1]
    wf_ = out_norm_fox.shape[1]
    ws_ = out_norm_sb.shape[1]
    hs = ws_ // hd
    n = batch * seq
    scale = hd ** -0.5
    fox_blk = _tile(seq, FOX_K_BLOCK)
    assert w_in.shape[2] == 3 * wf_ + hf + 3 * ws_ and hf * hd == wf_ and hf <= LANES

    w_fox = w_in[:, :, :3 * wf_].astype(BF16)
    w_sb = w_in[:, :, 3 * wf_ + hf:].astype(BF16)
    w_forget = jnp.pad(w_in[:, :, 3 * wf_:3 * wf_ + hf], ((0, 0), (0, 0), (0, LANES - hf))).astype(BF16)
    b_pad = jnp.pad(b_forget, ((0, 0), (0, LANES - hf)))
    ones_f = jnp.ones((depth, wf_), F32)
    ones_s = jnp.ones((depth, ws_), F32)
    colgain = jnp.concatenate([jnp.tile(q_norm * scale, (1, hf)), jnp.tile(k_norm, (1, hf)), ones_f,
                               ones_s * scale, ones_s, ones_s], axis=-1)
    w_out_b = w_out.astype(BF16)
    w_up_b = w_up.astype(BF16)
    w_down_b = w_down.astype(BF16)

    x2 = x.reshape(n, d)
    for l in range(depth):
        proj, f = _in_proj(x2, attn_norm[l][None], w_fox[l], w_sb[l], colgain[l][None], w_forget[l],
                           norm_cols=2 * wf_, head_dim=hd)
        c, qa, ka = _forget_cum(f.reshape(batch, seq, LANES), b_pad[l][None], n_heads=hf, head_dim=hd)
        c_heads = jnp.swapaxes(c[:, :, :hf], 1, 2).reshape(batch * hf, seq)
        c_edge = c_heads[:, ::fox_blk]
        qk_bound = (1.01 * hd * scale * jnp.max(jnp.abs(q_norm[l])) * jnp.max(jnp.abs(k_norm[l]))).reshape(1)
        proj3 = proj.reshape(batch, seq, proj.shape[1])
        out_a = _fox_attn(proj3, qa, ka, c_heads[:, :, None], c_edge, qk_bound, n_heads=hf, head_dim=hd,
                          q_off=0, k_off=hf, v_off=2 * hf, out_width=wf_)
        out_b = _sb_attn(proj3, n_heads=hs, head_dim=hd, q_off=3 * hf, k_off=3 * hf + hs,
                         v_off=3 * hf + 2 * hs, out_width=ws_)
        x2 = _out_proj(out_a.reshape(n, wf_), out_b.reshape(n, ws_), out_norm_fox[l][None],
                       out_norm_sb[l][None], w_out_b[l], x2)
        x2 = _conv_ffn(x2, ffn_norm[l][None], w_up_b[l], conv_w[l], conv_b[l][None], w_down_b[l], seq=seq)
    return x2.reshape(batch, seq, d)
```

```python
import functools
import math

import jax
import jax.numpy as jnp
from jax import lax
from jax.experimental import pallas as pl
from jax.experimental.pallas import tpu as pltpu

EPS = 1e-6
LANES = 128
SUBLANES = 8
MXU_DIM = 256
VMEM_LIMIT_BYTES = 60 * 1024 * 1024
MASK_VALUE = -1e30
EXP_UNDERFLOW = 90.0
FOX_Q_BLOCK = 512
FOX_K_BLOCK = 512
FOX_HEADS_PER_STEP = 2
SB_HEADS_PER_STEP = 4
FFN_ROW_CHAINS = 2
OUT_ROW_CHAINS = 2
IN_ROW_CHAINS = 2

F32 = jnp.float32
BF16 = jnp.bfloat16
_NT = (((1,), (1,)), ((), ()))


def _tile(dim, want):
    t = min(dim, want)
    while dim % t:
        t -= 1
    return t


def _heads_per_step(want, n_heads, *col_offsets):
    g = want
    while g > 1 and any(v % g for v in (n_heads,) + col_offsets):
        g //= 2
    return g


def _run_skewed(chains):
    results = [None] * len(chains)
    started = 0
    live = []
    while started < len(chains) or live:
        if started < len(chains):
            live.append(started)
            started += 1
        for c in list(live):
            try:
                next(chains[c])
            except StopIteration as done:
                results[c] = done.value
                live.remove(c)
    return results


def _params(*sem):
    return pltpu.CompilerParams(dimension_semantics=sem, vmem_limit_bytes=VMEM_LIMIT_BYTES)


def _rms(x, gain):
    ms = jnp.mean(x * x, axis=-1, keepdims=True)
    return x * lax.rsqrt(ms + EPS) * gain


def _in_proj_kernel(x_ref, g_ref, wa_ref, wb_ref, cg_ref, wf_ref, o_ref, f_ref, h_ref, *,
                    norm_tiles, a_tiles, head_dim, row_chains):
    j = pl.program_id(1)
    tm, tn = o_ref.shape

    @pl.when(j == 0)
    def _():
        h = _rms(x_ref[...], g_ref[...]).astype(BF16)
        h_ref[...] = h
        f_ref[...] = jnp.dot(h, wf_ref[...], preferred_element_type=F32)

    def rows(w_ref, normed, r0, nrows):
        rsl = slice(r0, r0 + nrows)
        acc = jnp.dot(h_ref[rsl, :], w_ref[...], preferred_element_type=F32)
        yield
        if normed:
            for g in range(tn // head_dim):
                sl = slice(g * head_dim, (g + 1) * head_dim)
                o_ref[rsl, sl] = _rms(acc[:, sl], cg_ref[:, sl]).astype(o_ref.dtype)
        else:
            o_ref[rsl, :] = (acc * cg_ref[...]).astype(o_ref.dtype)

    def tile(w_ref, normed):
        nrows = tm // row_chains
        _run_skewed([rows(w_ref, normed, c * nrows, nrows) for c in range(row_chains)])

    pl.when(j < norm_tiles)(lambda: tile(wa_ref, True))
    pl.when(jnp.logical_and(j >= norm_tiles, j < a_tiles))(lambda: tile(wa_ref, False))
    pl.when(j >= a_tiles)(lambda: tile(wb_ref, False))


def _in_proj(x2, gain, wa, wb, colgain, wf, *, norm_cols, head_dim):
    n, d = x2.shape
    ca, cb = wa.shape[1], wb.shape[1]
    tm = _tile(n, 1024)
    tn = _tile(math.gcd(norm_cols, ca, cb), 512)
    assert tn % head_dim == 0
    a_tiles = ca // tn
    row_chains = IN_ROW_CHAINS if tm % (IN_ROW_CHAINS * 2 * SUBLANES) == 0 else 1
    kern = functools.partial(_in_proj_kernel, norm_tiles=norm_cols // tn, a_tiles=a_tiles, head_dim=head_dim,
                             row_chains=row_chains)
    return pl.pallas_call(
        kern,
        grid=(n // tm, (ca + cb) // tn),
        in_specs=[
            pl.BlockSpec((tm, d), lambda i, j: (i, 0)),
            pl.BlockSpec((1, d), lambda i, j: (0, 0)),
            pl.BlockSpec((d, tn), lambda i, j: (0, jnp.minimum(j, a_tiles - 1))),
            pl.BlockSpec((d, tn), lambda i, j: (0, jnp.maximum(j - a_tiles, 0))),
            pl.BlockSpec((1, tn), lambda i, j: (0, j)),
            pl.BlockSpec((d, LANES), lambda i, j: (0, 0)),
        ],
        out_specs=[
            pl.BlockSpec((tm, tn), lambda i, j: (i, j)),
            pl.BlockSpec((tm, LANES), lambda i, j: (i, 0)),
        ],
        out_shape=[
            jax.ShapeDtypeStruct((n, ca + cb), BF16),
            jax.ShapeDtypeStruct((n, LANES), F32),
        ],
        scratch_shapes=[pltpu.VMEM((tm, d), BF16)],
        compiler_params=_params("parallel", "arbitrary"),
        name="in_proj",
    )(x2, gain, wa, wb, colgain, wf)


def _forget_cum_kernel(f_ref, b_ref, c_ref, qa_ref, ka_ref, carry_ref, *, n_heads, head_dim):
    @pl.when(pl.program_id(1) == 0)
    def _():
        carry_ref[...] = jnp.zeros_like(carry_ref)

    z = f_ref[0] + b_ref[...]
    v = jnp.minimum(z, 0.0) - jnp.log1p(jnp.exp(-jnp.abs(z)))
    rows = v.shape[0]
    row = lax.broadcasted_iota(jnp.int32, v.shape, 0)
    shift = 1
    while shift < rows:
        v = v + jnp.where(row >= shift, pltpu.roll(v, shift, axis=0), 0.0)
        shift *= 2
    v = v + carry_ref[...]
    c_ref[0] = v
    carry_ref[...] = v[rows - 1:rows, :]

    lane = lax.broadcasted_iota(jnp.int32, (rows, head_dim), 1)
    for h in range(n_heads):
        c = jnp.broadcast_to(v[:, h:h + 1], (rows, head_dim))
        c1 = c.astype(BF16).astype(F32)
        c2 = (c - c1).astype(BF16).astype(F32)
        c3 = c - c1 - c2
        ones = jnp.where(lane < 6, 1.0, 0.0)
        q_cols = jnp.where(lane == 0, c1, jnp.where(lane == 1, c2, jnp.where(lane == 2, c3, ones)))
        k_cols = jnp.where(lane == 3, -c1, jnp.where(lane == 4, -c2, jnp.where(lane == 5, -c3, ones)))
        sl = slice(h * head_dim, (h + 1) * head_dim)
        qa_ref[0, :, sl] = q_cols.astype(BF16)
        ka_ref[0, :, sl] = k_cols.astype(BF16)


def _forget_cum(f3, bias, *, n_heads, head_dim):
    b, s, _ = f3.shape
    tc = _tile(s, 1024)
    wide = n_heads * head_dim
    kern = functools.partial(_forget_cum_kernel, n_heads=n_heads, head_dim=head_dim)
    return pl.pallas_call(
        kern,
        grid=(b, s // tc),
        in_specs=[
            pl.BlockSpec((1, tc, LANES), lambda i, t: (i, t, 0)),
            pl.BlockSpec((1, LANES), lambda i, t: (0, 0)),
        ],
        out_specs=[
            pl.BlockSpec((1, tc, LANES), lambda i, t: (i, t, 0)),
            pl.BlockSpec((1, tc, wide), lambda i, t: (i, t, 0)),
            pl.BlockSpec((1, tc, wide), lambda i, t: (i, t, 0)),
        ],
        out_shape=[
            jax.ShapeDtypeStruct(f3.shape, F32),
            jax.ShapeDtypeStruct((b, s, wide), BF16),
            jax.ShapeDtypeStruct((b, s, wide), BF16),
        ],
        scratch_shapes=[pltpu.VMEM((1, LANES), F32)],
        compiler_params=_params("parallel", "arbitrary"),
        name="forget_cum",
    )(f3, bias)


def _fox_kernel(cedge_ref, qk_bound_ref, q_ref, qa_ref, k_ref, ka_ref, v_ref, cq_ref, o_ref, *,
                blk, kblk, hd, heads):
    bi, g, qi = pl.program_id(0), pl.program_id(1), pl.program_id(2)
    first_head = (bi * pl.num_programs(1) + g) * heads
    cols = [slice(h * hd, (h + 1) * hd) for h in range(heads)]
    qs = [jnp.concatenate([q_ref[0, :, c], qa_ref[0, :, c]], axis=1) for c in cols]

    def blocks(h, kbs, state):
        m, l, acc = state
        for kb, diag in kbs:
            start = pl.multiple_of(kb * kblk, kblk)
            k = jnp.concatenate([k_ref[0, pl.ds(start, kblk), cols[h]], ka_ref[0, pl.ds(start, kblk), cols[h]]],
                                axis=1)
            s = lax.dot_general(qs[h], k, _NT, preferred_element_type=F32)
            yield
            if diag is not None:
                s = jnp.where(lax.broadcasted_iota(jnp.int32, s.shape, 1) + diag
                              <= lax.broadcasted_iota(jnp.int32, s.shape, 0), s, MASK_VALUE)
            m_new = jnp.maximum(m, jnp.max(s, axis=-1, keepdims=True))
            alpha = jnp.exp(m - m_new)
            p = jnp.exp(s - m_new)
            l = alpha * l + jnp.sum(p, axis=-1, keepdims=True)
            p_b = p.astype(BF16)
            m = m_new
            yield
            v = v_ref[0, pl.ds(start, kblk), cols[h]]
            acc = alpha * acc + jnp.dot(p_b, v, preferred_element_type=F32)
        return m, l, acc

    def more_left(kb, ms):
        slack = [jnp.max(cq_ref[h] - ms[h]) - cedge_ref[first_head + h, kb] for h in range(heads)]
        return functools.reduce(jnp.maximum, slack) + qk_bound_ref[0] > -EXP_UNDERFLOW

    per_q = blk // kblk
    first_kb = qi * per_q
    zeros = jnp.zeros((blk, 1), F32)
    init = (zeros + MASK_VALUE, zeros, jnp.zeros((blk, hd), F32))
    state = _run_skewed([blocks(h, [(first_kb + j, j * kblk) for j in range(per_q)], init) for h in range(heads)])
    ms, ls, accs = (tuple(s[i] for s in state) for i in range(3))

    def cond(c):
        t, go = c[0], c[1]
        return jnp.logical_and(t <= first_kb, go)

    def body(c):
        t, _, ms, ls, accs = c
        kb = first_kb - t
        state = _run_skewed([blocks(h, [(kb, None)], (ms[h], ls[h], accs[h])) for h in range(heads)])
        ms, ls, accs = (tuple(s[i] for s in state) for i in range(3))
        return t + 1, more_left(kb, ms), ms, ls, accs

    _, _, _, ls, accs = lax.while_loop(cond, body, (jnp.int32(1), more_left(first_kb, ms), ms, ls, accs))
    for h in range(heads):
        o_ref[0, :, cols[h]] = accs[h] / ls[h]


def _fox_attn(proj3, qa3, ka3, c_col, cedge, qk_bound, *, n_heads, head_dim, q_off, k_off, v_off, out_width):
    b, s, _ = proj3.shape
    kblk = s // cedge.shape[1]
    blk = _tile(s, FOX_Q_BLOCK)
    heads = _heads_per_step(FOX_HEADS_PER_STEP, n_heads, q_off, k_off, v_off)
    wide = heads * head_dim
    groups = n_heads // heads
    kern = functools.partial(_fox_kernel, blk=blk, kblk=kblk, hd=head_dim, heads=heads)
    grid_spec = pltpu.PrefetchScalarGridSpec(
        num_scalar_prefetch=2,
        grid=(b, groups, s // blk),
        in_specs=[
            pl.BlockSpec((1, blk, wide), lambda bi, g, qi, *_: (bi, qi, q_off // heads + g)),
            pl.BlockSpec((1, blk, wide), lambda bi, g, qi, *_: (bi, qi, g)),
            pl.BlockSpec((1, s, wide), lambda bi, g, qi, *_: (bi, 0, k_off // heads + g)),
            pl.BlockSpec((1, s, wide), lambda bi, g, qi, *_: (bi, 0, g)),
            pl.BlockSpec((1, s, wide), lambda bi, g, qi, *_: (bi, 0, v_off // heads + g)),
            pl.BlockSpec((heads, blk, 1), lambda bi, g, qi, *_: (bi * groups + g, qi, 0)),
        ],
        out_specs=pl.BlockSpec((1, blk, wide), lambda bi, g, qi, *_: (bi, qi, g)),
    )
    return pl.pallas_call(
        kern,
        grid_spec=grid_spec,
        out_shape=jax.ShapeDtypeStruct((b, s, out_width), F32),
        compiler_params=_params("parallel", "parallel", "arbitrary"),
        name="fox_attn",
    )(cedge, qk_bound, proj3, qa3, proj3, ka3, proj3, c_col)


def _sb_kernel(q_ref, k_ref, v_ref, o_ref, *, blk, hd, heads):
    qi = pl.program_id(2)
    tri = (lax.broadcasted_iota(jnp.int32, (blk, blk), 0)
           > lax.broadcasted_iota(jnp.int32, (blk, blk), 1)).astype(BF16)
    qs = [q_ref[0, :, h * hd:(h + 1) * hd] for h in range(heads)]

    def block(h, kb, masked):
        start = pl.multiple_of(kb * blk, blk)
        k = k_ref[0, pl.ds(start, blk), h * hd:(h + 1) * hd]
        z = lax.dot_general(qs[h], k, _NT, preferred_element_type=F32)
        yield
        sp = jnp.maximum(z, 0.0) + jnp.log(1.0 + jnp.exp(-jnp.abs(z)))
        log_beta = z - sp
        if masked:
            allowed = (lax.broadcasted_iota(jnp.int32, z.shape, 1)
                       < lax.broadcasted_iota(jnp.int32, z.shape, 0))
            sp = jnp.where(allowed, sp, 0.0)
        sp_sum = jnp.sum(sp, axis=-1, keepdims=True)
        sp_b = sp.astype(BF16)
        yield
        inner = jnp.dot(sp_b, tri, preferred_element_type=F32)
        yield
        w = jnp.exp(log_beta - inner)
        if masked:
            w = jnp.where(allowed, w, 0.0)
        w_b = w.astype(BF16)
        yield
        v = v_ref[0, pl.ds(start, blk), h * hd:(h + 1) * hd]
        return sp_sum, jnp.dot(w_b, v, preferred_element_type=F32)

    def alive(rights):
        return functools.reduce(jnp.maximum, [jnp.max(jnp.exp(-r)) for r in rights]) > 0.0

    has_left = (qi > 0).astype(F32)
    left = jnp.maximum(qi - 1, 0)
    first = _run_skewed([blk_ for h in range(heads) for blk_ in (block(h, qi, True), block(h, left, False))])
    rights, accs = [], []
    for h in range(heads):
        (sum_d, pv_d), (sum_l, pv_l) = first[2 * h], first[2 * h + 1]
        accs.append(pv_d + (has_left * jnp.exp(-sum_d)) * pv_l)
        rights.append(sum_d + sum_l)
    rights, accs = tuple(rights), tuple(accs)

    def cond(c):
        t, go, _, _ = c
        return jnp.logical_and(t < qi, go)

    def body(c):
        t, _, rights, accs = c
        new = _run_skewed([block(h, qi - 1 - t, False) for h in range(heads)])
        accs = tuple(accs[h] + jnp.exp(-rights[h]) * new[h][1] for h in range(heads))
        rights = tuple(rights[h] + new[h][0] for h in range(heads))
        return t + 1, alive(rights), rights, accs

    _, _, _, accs = lax.while_loop(cond, body, (jnp.int32(1), alive(rights), rights, accs))
    for h in range(heads):
        o_ref[0, :, h * hd:(h + 1) * hd] = accs[h]


def _sb_attn(proj3, *, n_heads, head_dim, q_off, k_off, v_off, out_width):
    b, s, _ = proj3.shape
    blk = _tile(s, MXU_DIM)
    heads = _heads_per_step(SB_HEADS_PER_STEP, n_heads, q_off, k_off, v_off)
    wide = heads * head_dim
    kern = functools.partial(_sb_kernel, blk=blk, hd=head_dim, heads=heads)
    return pl.pallas_call(
        kern,
        grid=(b, n_heads // heads, s // blk),
        in_specs=[
            pl.BlockSpec((1, blk, wide), lambda bi, g, qi: (bi, qi, q_off // heads + g)),
            pl.BlockSpec((1, s, wide), lambda bi, g, qi: (bi, 0, k_off // heads + g)),
            pl.BlockSpec((1, s, wide), lambda bi, g, qi: (bi, 0, v_off // heads + g)),
        ],
        out_specs=pl.BlockSpec((1, blk, wide), lambda bi, g, qi: (bi, qi, g)),
        out_shape=jax.ShapeDtypeStruct((b, s, out_width), F32),
        compiler_params=_params("parallel", "parallel", "arbitrary"),
        name="sb_attn",
    )(proj3, proj3, proj3)


def _out_proj_kernel(oa_ref, ob_ref, ga_ref, gb_ref, w_ref, x_ref, o_ref, *, row_chains):
    def rows(r0, nrows):
        sl = slice(r0, r0 + nrows)
        merged = jnp.concatenate([_rms(oa_ref[sl, :], ga_ref[...]).astype(BF16),
                                  _rms(ob_ref[sl, :], gb_ref[...]).astype(BF16)], axis=1)
        yield
        o_ref[sl, :] = x_ref[sl, :] + jnp.dot(merged, w_ref[...], preferred_element_type=F32)

    nrows = o_ref.shape[0] // row_chains
    _run_skewed([rows(c * nrows, nrows) for c in range(row_chains)])


def _out_proj(oa, ob, ga, gb, w, x2):
    n, d = x2.shape
    wa, wb = oa.shape[1], ob.shape[1]
    tm = _tile(n, 512)
    row_chains = OUT_ROW_CHAINS if tm % (OUT_ROW_CHAINS * SUBLANES) == 0 else 1
    return pl.pallas_call(
        functools.partial(_out_proj_kernel, row_chains=row_chains),
        grid=(n // tm,),
        in_specs=[
            pl.BlockSpec((tm, wa), lambda i: (i, 0)),
            pl.BlockSpec((tm, wb), lambda i: (i, 0)),
            pl.BlockSpec((1, wa), lambda i: (0, 0)),
            pl.BlockSpec((1, wb), lambda i: (0, 0)),
            pl.BlockSpec((wa + wb, d), lambda i: (0, 0)),
            pl.BlockSpec((tm, d), lambda i: (i, 0)),
        ],
        out_specs=pl.BlockSpec((tm, d), lambda i: (i, 0)),
        out_shape=jax.ShapeDtypeStruct((n, d), F32),
        compiler_params=_params("parallel"),
        name="out_proj",
    )(oa, ob, ga, gb, w, x2)


def _conv_ffn_kernel(x_ref, xp_ref, g_ref, wg_ref, wv_ref, cwg_ref, cwv_ref, cbg_ref, cbv_ref, wd_ref,
                     o_ref, h_ref, ug_ref, uv_ref, *, tiles_per_seq, row_chains):
    i = pl.program_id(0)
    tm = x_ref.shape[0]
    halo = xp_ref.shape[1]

    @pl.when(pl.program_id(1) == 0)
    def _():
        keep = (i % tiles_per_seq != 0).astype(F32)
        h_ref[:halo, :] = _rms(xp_ref[0] * keep, g_ref[...]).astype(BF16)
        h_ref[halo:, :] = _rms(x_ref[...], g_ref[...]).astype(BF16)
        o_ref[...] = x_ref[...]

    def conv(u_ref, nrows, cw_ref, cb_ref):
        cw = cw_ref[...]
        return (cw[2:3, :] * u_ref[halo:halo + nrows, :] + cw[1:2, :] * u_ref[halo - 1:halo - 1 + nrows, :]
                + cw[0:1, :] * u_ref[halo - 2:halo - 2 + nrows, :] + cb_ref[...])

    def rows(c, r0, nrows):
        h = h_ref[r0:r0 + halo + nrows, :]
        ug_ref[c] = jnp.dot(h, wg_ref[...], preferred_element_type=F32)
        uv_ref[c] = jnp.dot(h, wv_ref[...], preferred_element_type=F32)
        yield
        gate = conv(ug_ref.at[c], nrows, cwg_ref, cbg_ref)
        val = conv(uv_ref.at[c], nrows, cwv_ref, cbv_ref)
        act = (gate * (1.0 / (1.0 + jnp.exp(-gate))) * val).astype(BF16)
        yield
        o_ref[r0:r0 + nrows, :] += jnp.dot(act, wd_ref[...], preferred_element_type=F32)

    nrows = tm // row_chains
    _run_skewed([rows(c, c * nrows, nrows) for c in range(row_chains)])


def _conv_ffn(x2, gain, w_up, conv_w, conv_b, w_down, *, seq):
    n, d = x2.shape
    f = w_down.shape[0]
    tm = _tile(seq, 1024)
    tn = _tile(f, 512)
    halo = SUBLANES
    fb = f // tn
    xp = x2.reshape(n // halo, halo, d)
    row_chains = FFN_ROW_CHAINS if tm % (FFN_ROW_CHAINS * SUBLANES) == 0 else 1
    kern = functools.partial(_conv_ffn_kernel, tiles_per_seq=seq // tm, row_chains=row_chains)
    return pl.pallas_call(
        kern,
        grid=(n // tm, fb),
        in_specs=[
            pl.BlockSpec((tm, d), lambda i, j: (i, 0), pipeline_mode=pl.Buffered(1)),
            pl.BlockSpec((1, halo, d), lambda i, j: (jnp.maximum(i * (tm // halo) - 1, 0), 0, 0)),
            pl.BlockSpec((1, d), lambda i, j: (0, 0)),
            pl.BlockSpec((d, tn), lambda i, j: (0, j)),
            pl.BlockSpec((d, tn), lambda i, j: (0, j + fb)),
            pl.BlockSpec((conv_w.shape[0], tn), lambda i, j: (0, j)),
            pl.BlockSpec((conv_w.shape[0], tn), lambda i, j: (0, j + fb)),
            pl.BlockSpec((1, tn), lambda i, j: (0, j)),
            pl.BlockSpec((1, tn), lambda i, j: (0, j + fb)),
            pl.BlockSpec((tn, d), lambda i, j: (j, 0)),
        ],
        out_specs=pl.BlockSpec((tm, d), lambda i, j: (i, 0)),
        out_shape=jax.ShapeDtypeStruct((n, d), F32),
        scratch_shapes=[pltpu.VMEM((halo + tm, d), BF16),
                        pltpu.VMEM((row_chains, halo + tm // row_chains, tn), F32),
                        pltpu.VMEM((row_chains, halo + tm // row_chains, tn), F32)],
        compiler_params=_params("parallel", "arbitrary"),
        name="conv_ffn",
    )(x2, xp, gain, w_up, w_up, conv_w, conv_w, conv_b, conv_b, w_down)


def kernel(x, attn_norm, w_in, b_forget, q_norm, k_norm, out_norm_fox, out_norm_sb, w_out, ffn_norm,
           w_up, conv_w, conv_b, w_down):
    batch, seq, d = x.shape
    depth = w_in.shape[0]
    hf = b_forget.shape[1]
    hd = q_norm.shape[1]
    wf_ = out_norm_fox.shape[1]
    ws_ = out_norm_sb.shape[1]
    hs = ws_ // hd
    n = batch * seq
    scale = hd ** -0.5
    fox_blk = _tile(seq, FOX_K_BLOCK)
    assert w_in.shape[2] == 3 * wf_ + hf + 3 * ws_ and hf * hd == wf_ and hf <= LANES

    w_in_b = w_in.astype(BF16)
    w_fox = w_in_b[:, :, :3 * wf_]
    w_sb = w_in_b[:, :, 3 * wf_ + hf:]
    w_forget = jnp.pad(w_in_b[:, :, 3 * wf_:3 * wf_ + hf], ((0, 0), (0, 0), (0, LANES - hf)))
    b_pad = jnp.pad(b_forget, ((0, 0), (0, LANES - hf)))
    ones_f = jnp.ones((depth, wf_), F32)
    ones_s = jnp.ones((depth, ws_), F32)
    colgain = jnp.concatenate([jnp.tile(q_norm * scale, (1, hf)), jnp.tile(k_norm, (1, hf)), ones_f,
                               ones_s * scale, ones_s, ones_s], axis=-1)
    w_out_b = w_out.astype(BF16)
    w_up_b = w_up.astype(BF16)
    w_down_b = w_down.astype(BF16)

    x2 = x.reshape(n, d)
    for l in range(depth):
        proj, f = _in_proj(x2, attn_norm[l][None], w_fox[l], w_sb[l], colgain[l][None], w_forget[l],
                           norm_cols=2 * wf_, head_dim=hd)
        c, qa, ka = _forget_cum(f.reshape(batch, seq, LANES), b_pad[l][None], n_heads=hf, head_dim=hd)
        c_heads = jnp.swapaxes(c[:, :, :hf], 1, 2).reshape(batch * hf, seq)
        c_edge = c_heads[:, ::fox_blk]
        qk_bound = (1.01 * hd * scale * jnp.max(jnp.abs(q_norm[l])) * jnp.max(jnp.abs(k_norm[l]))).reshape(1)
        proj3 = proj.reshape(batch, seq, proj.shape[1])
        out_a = _fox_attn(proj3, qa, ka, c_heads[:, :, None], c_edge, qk_bound, n_heads=hf, head_dim=hd,
                          q_off=0, k_off=hf, v_off=2 * hf, out_width=wf_)
        out_b = _sb_attn(proj3, n_heads=hs, head_dim=hd, q_off=3 * hf, k_off=3 * hf + hs,
                         v_off=3 * hf + 2 * hs, out_width=ws_)
        x2 = _out_proj(out_a.reshape(n, wf_), out_b.reshape(n, ws_), out_norm_fox[l][None],
                       out_norm_sb[l][None], w_out_b[l], x2)
        x2 = _conv_ffn(x2, ffn_norm[l][None], w_up_b[l], conv_w[l], conv_b[l][None], w_down_b[l], seq=seq)
    return x2.reshape(batch, seq, d)
```

```python
import functools
import math

import jax
import jax.numpy as jnp
from jax import lax
from jax.experimental import pallas as pl
from jax.experimental.pallas import tpu as pltpu

EPS = 1e-6
LANES = 128
SUBLANES = 8
MXU_DIM = 256
VMEM_LIMIT_BYTES = 60 * 1024 * 1024
MASK_VALUE = -1e30
EXP_UNDERFLOW = 90.0
FOX_Q_BLOCK = 512
FOX_K_BLOCK = 512
FOX_HEADS_PER_STEP = 2
SB_HEADS_PER_STEP = 4
FFN_ROW_CHAINS = 2
OUT_ROW_CHAINS = 2
IN_ROW_CHAINS = 2

F32 = jnp.float32
BF16 = jnp.bfloat16
_NT = (((1,), (1,)), ((), ()))
_TN = (((0,), (0,)), ((), ()))


def _tile(dim, want):
    t = min(dim, want)
    while dim % t:
        t -= 1
    return t


def _heads_per_step(want, n_heads, *col_offsets):
    g = want
    while g > 1 and any(v % g for v in (n_heads,) + col_offsets):
        g //= 2
    return g


def _run_skewed(chains):
    results = [None] * len(chains)
    started = 0
    live = []
    while started < len(chains) or live:
        if started < len(chains):
            live.append(started)
            started += 1
        for c in list(live):
            try:
                next(chains[c])
            except StopIteration as done:
                results[c] = done.value
                live.remove(c)
    return results


def _params(*sem):
    return pltpu.CompilerParams(dimension_semantics=sem, vmem_limit_bytes=VMEM_LIMIT_BYTES)


def _rms(x, gain):
    ms = jnp.mean(x * x, axis=-1, keepdims=True)
    return x * lax.rsqrt(ms + EPS) * gain


def _in_proj_kernel(x_ref, g_ref, wa_ref, wb_ref, cg_ref, wf_ref, o_ref, f_ref, h_ref, *,
                    norm_tiles, a_tiles, head_dim, row_chains):
    j = pl.program_id(1)
    tm, tn = o_ref.shape

    @pl.when(j == 0)
    def _():
        h = _rms(x_ref[...], g_ref[...]).astype(BF16)
        h_ref[...] = h
        f_ref[...] = jnp.dot(h, wf_ref[...], preferred_element_type=F32)

    def rows(w_ref, normed, r0, nrows):
        rsl = slice(r0, r0 + nrows)
        acc = jnp.dot(h_ref[rsl, :], w_ref[...], preferred_element_type=F32)
        yield
        if normed:
            for g in range(tn // head_dim):
                sl = slice(g * head_dim, (g + 1) * head_dim)
                o_ref[rsl, sl] = _rms(acc[:, sl], cg_ref[:, sl]).astype(o_ref.dtype)
        else:
            o_ref[rsl, :] = (acc * cg_ref[...]).astype(o_ref.dtype)

    def tile(w_ref, normed):
        nrows = tm // row_chains
        _run_skewed([rows(w_ref, normed, c * nrows, nrows) for c in range(row_chains)])

    pl.when(j < norm_tiles)(lambda: tile(wa_ref, True))
    pl.when(jnp.logical_and(j >= norm_tiles, j < a_tiles))(lambda: tile(wa_ref, False))
    pl.when(j >= a_tiles)(lambda: tile(wb_ref, False))


def _in_proj(x2, gain, wa, wb, colgain, wf, *, layer, norm_cols, head_dim):
    n, d = x2.shape
    ca, cb = wa.shape[2], wb.shape[2]
    tm = _tile(n, 1024)
    tn = _tile(math.gcd(norm_cols, ca, cb), 512)
    assert tn % head_dim == 0
    a_tiles = ca // tn
    row_chains = IN_ROW_CHAINS if tm % (IN_ROW_CHAINS * 2 * SUBLANES) == 0 else 1
    kern = functools.partial(_in_proj_kernel, norm_tiles=norm_cols // tn, a_tiles=a_tiles, head_dim=head_dim,
                             row_chains=row_chains)
    return pl.pallas_call(
        kern,
        grid=(n // tm, (ca + cb) // tn),
        in_specs=[
            pl.BlockSpec((tm, d), lambda i, j: (i, 0)),
            pl.BlockSpec((1, d), lambda i, j: (0, 0)),
            pl.BlockSpec((None, d, tn), lambda i, j: (layer, 0, jnp.minimum(j, a_tiles - 1))),
            pl.BlockSpec((None, d, tn), lambda i, j: (layer, 0, jnp.maximum(j - a_tiles, 0))),
            pl.BlockSpec((1, tn), lambda i, j: (0, j)),
            pl.BlockSpec((None, d, LANES), lambda i, j: (layer, 0, 0)),
        ],
        out_specs=[
            pl.BlockSpec((tm, tn), lambda i, j: (i, j)),
            pl.BlockSpec((tm, LANES), lambda i, j: (i, 0)),
        ],
        out_shape=[
            jax.ShapeDtypeStruct((n, ca + cb), BF16),
            jax.ShapeDtypeStruct((n, LANES), F32),
        ],
        scratch_shapes=[pltpu.VMEM((tm, d), BF16)],
        compiler_params=_params("parallel", "arbitrary"),
        name="in_proj",
    )(x2, gain, wa, wb, colgain, wf)


def _forget_cum_kernel(f_ref, b_ref, c_ref, qa_ref, ka_ref, carry_ref, *, n_heads, head_dim):
    @pl.when(pl.program_id(1) == 0)
    def _():
        carry_ref[...] = jnp.zeros_like(carry_ref)

    z = f_ref[0] + b_ref[...]
    v = jnp.minimum(z, 0.0) - jnp.log1p(jnp.exp(-jnp.abs(z)))
    rows = v.shape[0]
    row = lax.broadcasted_iota(jnp.int32, v.shape, 0)
    shift = 1
    while shift < rows:
        v = v + jnp.where(row >= shift, pltpu.roll(v, shift, axis=0), 0.0)
        shift *= 2
    v = v + carry_ref[...]
    c_ref[0] = v
    carry_ref[...] = v[rows - 1:rows, :]

    lane = lax.broadcasted_iota(jnp.int32, (rows, head_dim), 1)
    for h in range(n_heads):
        c = jnp.broadcast_to(v[:, h:h + 1], (rows, head_dim))
        c1 = c.astype(BF16).astype(F32)
        c2 = (c - c1).astype(BF16).astype(F32)
        c3 = c - c1 - c2
        ones = jnp.where(lane < 6, 1.0, 0.0)
        q_cols = jnp.where(lane == 0, c1, jnp.where(lane == 1, c2, jnp.where(lane == 2, c3, ones)))
        k_cols = jnp.where(lane == 3, -c1, jnp.where(lane == 4, -c2, jnp.where(lane == 5, -c3, ones)))
        sl = slice(h * head_dim, (h + 1) * head_dim)
        qa_ref[0, :, sl] = q_cols.astype(BF16)
        ka_ref[0, :, sl] = k_cols.astype(BF16)


def _forget_cum(f3, bias, *, n_heads, head_dim):
    b, s, _ = f3.shape
    tc = _tile(s, 1024)
    wide = n_heads * head_dim
    kern = functools.partial(_forget_cum_kernel, n_heads=n_heads, head_dim=head_dim)
    return pl.pallas_call(
        kern,
        grid=(b, s // tc),
        in_specs=[
            pl.BlockSpec((1, tc, LANES), lambda i, t: (i, t, 0)),
            pl.BlockSpec((1, LANES), lambda i, t: (0, 0)),
        ],
        out_specs=[
            pl.BlockSpec((1, tc, LANES), lambda i, t: (i, t, 0)),
            pl.BlockSpec((1, tc, wide), lambda i, t: (i, t, 0)),
            pl.BlockSpec((1, tc, wide), lambda i, t: (i, t, 0)),
        ],
        out_shape=[
            jax.ShapeDtypeStruct(f3.shape, F32),
            jax.ShapeDtypeStruct((b, s, wide), BF16),
            jax.ShapeDtypeStruct((b, s, wide), BF16),
        ],
        scratch_shapes=[pltpu.VMEM((1, LANES), F32)],
        compiler_params=_params("parallel", "arbitrary"),
        name="forget_cum",
    )(f3, bias)


def _fox_kernel(cedge_ref, qk_bound_ref, q_ref, qa_ref, k_ref, ka_ref, v_ref, cq_ref, o_ref, *,
                blk, kblk, hd, heads):
    bi, g, qi = pl.program_id(0), pl.program_id(1), pl.program_id(2)
    first_head = (bi * pl.num_programs(1) + g) * heads
    cols = [slice(h * hd, (h + 1) * hd) for h in range(heads)]
    qs = [jnp.concatenate([q_ref[0, :, c], qa_ref[0, :, c]], axis=1) for c in cols]

    def blocks(h, kbs, state):
        m, l, acc = state
        for kb, diag in kbs:
            start = pl.multiple_of(kb * kblk, kblk)
            k = jnp.concatenate([k_ref[0, pl.ds(start, kblk), cols[h]], ka_ref[0, pl.ds(start, kblk), cols[h]]],
                                axis=1)
            s = lax.dot_general(k, qs[h], _NT, preferred_element_type=F32)
            yield
            if diag is not None:
                s = jnp.where(lax.broadcasted_iota(jnp.int32, s.shape, 0) + diag
                              <= lax.broadcasted_iota(jnp.int32, s.shape, 1), s, MASK_VALUE)
            m_new = jnp.maximum(m, jnp.max(s, axis=0, keepdims=True))
            alpha = jnp.exp(m - m_new)
            p = jnp.exp(s - m_new)
            l = alpha * l + jnp.sum(p, axis=0, keepdims=True)
            p_b = p.astype(BF16)
            m = m_new
            yield
            v = v_ref[0, pl.ds(start, kblk), cols[h]]
            acc = alpha * acc + lax.dot_general(v, p_b, _TN, preferred_element_type=F32)
        return m, l, acc

    def more_left(kb, ms):
        slack = [jnp.max(cq_ref[h] - ms[h]) - cedge_ref[first_head + h, kb] for h in range(heads)]
        return functools.reduce(jnp.maximum, slack) + qk_bound_ref[0] > -EXP_UNDERFLOW

    per_q = blk // kblk
    first_kb = qi * per_q
    zeros = jnp.zeros((1, blk), F32)
    init = (zeros + MASK_VALUE, zeros, jnp.zeros((hd, blk), F32))
    state = _run_skewed([blocks(h, [(first_kb + j, j * kblk) for j in range(per_q)], init) for h in range(heads)])
    ms, ls, accs = (tuple(s[i] for s in state) for i in range(3))

    def cond(c):
        t, go = c[0], c[1]
        return jnp.logical_and(t <= first_kb, go)

    def body(c):
        t, _, ms, ls, accs = c
        kb = first_kb - t
        go = more_left(kb, ms)
        state = _run_skewed([blocks(h, [(kb, None)], (ms[h], ls[h], accs[h])) for h in range(heads)])
        ms, ls, accs = (tuple(s[i] for s in state) for i in range(3))
        return t + 1, go, ms, ls, accs

    _, _, _, ls, accs = lax.while_loop(cond, body, (jnp.int32(1), more_left(first_kb, ms), ms, ls, accs))
    for h in range(heads):
        o_ref[0, :, cols[h]] = jnp.transpose(accs[h] / ls[h])


def _fox_attn(proj3, qa3, ka3, c_row, cedge, qk_bound, *, n_heads, head_dim, q_off, k_off, v_off, out_width):
    b, s, _ = proj3.shape
    kblk = s // cedge.shape[1]
    blk = _tile(s, FOX_Q_BLOCK)
    heads = _heads_per_step(FOX_HEADS_PER_STEP, n_heads, q_off, k_off, v_off)
    wide = heads * head_dim
    groups = n_heads // heads
    kern = functools.partial(_fox_kernel, blk=blk, kblk=kblk, hd=head_dim, heads=heads)
    grid_spec = pltpu.PrefetchScalarGridSpec(
        num_scalar_prefetch=2,
        grid=(b, groups, s // blk),
        in_specs=[
            pl.BlockSpec((1, blk, wide), lambda bi, g, qi, *_: (bi, qi, q_off // heads + g)),
            pl.BlockSpec((1, blk, wide), lambda bi, g, qi, *_: (bi, qi, g)),
            pl.BlockSpec((1, s, wide), lambda bi, g, qi, *_: (bi, 0, k_off // heads + g)),
            pl.BlockSpec((1, s, wide), lambda bi, g, qi, *_: (bi, 0, g)),
            pl.BlockSpec((1, s, wide), lambda bi, g, qi, *_: (bi, 0, v_off // heads + g)),
            pl.BlockSpec((heads, 1, blk), lambda bi, g, qi, *_: (bi * groups + g, 0, qi)),
        ],
        out_specs=pl.BlockSpec((1, blk, wide), lambda bi, g, qi, *_: (bi, qi, g)),
    )
    return pl.pallas_call(
        kern,
        grid_spec=grid_spec,
        out_shape=jax.ShapeDtypeStruct((b, s, out_width), F32),
        compiler_params=_params("parallel", "parallel", "arbitrary"),
        name="fox_attn",
    )(cedge, qk_bound, proj3, qa3, proj3, ka3, proj3, c_row)


def _sb_kernel(q_ref, k_ref, v_ref, o_ref, *, blk, hd, heads):
    qi = pl.program_id(2)
    tri = (lax.broadcasted_iota(jnp.int32, (blk, blk), 0)
           > lax.broadcasted_iota(jnp.int32, (blk, blk), 1)).astype(BF16)
    qs = [q_ref[0, :, h * hd:(h + 1) * hd] for h in range(heads)]

    def block(h, kb, masked):
        start = pl.multiple_of(kb * blk, blk)
        k = k_ref[0, pl.ds(start, blk), h * hd:(h + 1) * hd]
        z = lax.dot_general(qs[h], k, _NT, preferred_element_type=F32)
        yield
        sp = jnp.maximum(z, 0.0) + jnp.log(1.0 + jnp.exp(-jnp.abs(z)))
        log_beta = z - sp
        if masked:
            allowed = (lax.broadcasted_iota(jnp.int32, z.shape, 1)
                       < lax.broadcasted_iota(jnp.int32, z.shape, 0))
            sp = jnp.where(allowed, sp, 0.0)
        sp_sum = jnp.sum(sp, axis=-1, keepdims=True)
        sp_b = sp.astype(BF16)
        yield
        inner = jnp.dot(sp_b, tri, preferred_element_type=F32)
        yield
        w = jnp.exp(log_beta - inner)
        if masked:
            w = jnp.where(allowed, w, 0.0)
        w_b = w.astype(BF16)
        yield
        v = v_ref[0, pl.ds(start, blk), h * hd:(h + 1) * hd]
        return sp_sum, jnp.dot(w_b, v, preferred_element_type=F32)

    def alive(rights):
        return functools.reduce(jnp.maximum, [jnp.max(jnp.exp(-r)) for r in rights]) > 0.0

    has_left = (qi > 0).astype(F32)
    left = jnp.maximum(qi - 1, 0)
    first = _run_skewed([blk_ for h in range(heads) for blk_ in (block(h, qi, True), block(h, left, False))])
    rights, accs = [], []
    for h in range(heads):
        (sum_d, pv_d), (sum_l, pv_l) = first[2 * h], first[2 * h + 1]
        accs.append(pv_d + (has_left * jnp.exp(-sum_d)) * pv_l)
        rights.append(sum_d + sum_l)
    rights, accs = tuple(rights), tuple(accs)

    def cond(c):
        t, go, _, _ = c
        return jnp.logical_and(t < qi, go)

    def body(c):
        t, _, rights, accs = c
        new = _run_skewed([block(h, qi - 1 - t, False) for h in range(heads)])
        accs = tuple(accs[h] + jnp.exp(-rights[h]) * new[h][1] for h in range(heads))
        rights = tuple(rights[h] + new[h][0] for h in range(heads))
        return t + 1, alive(rights), rights, accs

    _, _, _, accs = lax.while_loop(cond, body, (jnp.int32(1), alive(rights), rights, accs))
    for h in range(heads):
        o_ref[0, :, h * hd:(h + 1) * hd] = accs[h]


def _sb_attn(proj3, *, n_heads, head_dim, q_off, k_off, v_off, out_width):
    b, s, _ = proj3.shape
    blk = _tile(s, MXU_DIM)
    heads = _heads_per_step(SB_HEADS_PER_STEP, n_heads, q_off, k_off, v_off)
    wide = heads * head_dim
    kern = functools.partial(_sb_kernel, blk=blk, hd=head_dim, heads=heads)
    return pl.pallas_call(
        kern,
        grid=(b, n_heads // heads, s // blk),
        in_specs=[
            pl.BlockSpec((1, blk, wide), lambda bi, g, qi: (bi, qi, q_off // heads + g)),
            pl.BlockSpec((1, s, wide), lambda bi, g, qi: (bi, 0, k_off // heads + g)),
            pl.BlockSpec((1, s, wide), lambda bi, g, qi: (bi, 0, v_off // heads + g)),
        ],
        out_specs=pl.BlockSpec((1, blk, wide), lambda bi, g, qi: (bi, qi, g)),
        out_shape=jax.ShapeDtypeStruct((b, s, out_width), F32),
        compiler_params=_params("parallel", "parallel", "arbitrary"),
        name="sb_attn",
    )(proj3, proj3, proj3)


def _out_proj_kernel(oa_ref, ob_ref, ga_ref, gb_ref, w_ref, x_ref, o_ref, *, row_chains):
    def rows(r0, nrows):
        sl = slice(r0, r0 + nrows)
        merged = jnp.concatenate([_rms(oa_ref[sl, :], ga_ref[...]).astype(BF16),
                                  _rms(ob_ref[sl, :], gb_ref[...]).astype(BF16)], axis=1)
        yield
        o_ref[sl, :] = x_ref[sl, :] + jnp.dot(merged, w_ref[...], preferred_element_type=F32)

    nrows = o_ref.shape[0] // row_chains
    _run_skewed([rows(c * nrows, nrows) for c in range(row_chains)])


def _out_proj(oa, ob, ga, gb, w, x2, *, layer):
    n, d = x2.shape
    wa, wb = oa.shape[1], ob.shape[1]
    tm = _tile(n, 512)
    row_chains = OUT_ROW_CHAINS if tm % (OUT_ROW_CHAINS * SUBLANES) == 0 else 1
    return pl.pallas_call(
        functools.partial(_out_proj_kernel, row_chains=row_chains),
        grid=(n // tm,),
        in_specs=[
            pl.BlockSpec((tm, wa), lambda i: (i, 0)),
            pl.BlockSpec((tm, wb), lambda i: (i, 0)),
            pl.BlockSpec((1, wa), lambda i: (0, 0)),
            pl.BlockSpec((1, wb), lambda i: (0, 0)),
            pl.BlockSpec((None, wa + wb, d), lambda i: (layer, 0, 0)),
            pl.BlockSpec((tm, d), lambda i: (i, 0)),
        ],
        out_specs=pl.BlockSpec((tm, d), lambda i: (i, 0)),
        out_shape=jax.ShapeDtypeStruct((n, d), F32),
        compiler_params=_params("parallel"),
        name="out_proj",
    )(oa, ob, ga, gb, w, x2)


def _conv_ffn_kernel(x_ref, xp_ref, g_ref, wg_ref, wv_ref, cwg_ref, cwv_ref, cbg_ref, cbv_ref, wd_ref,
                     o_ref, h_ref, ug_ref, uv_ref, *, tiles_per_seq, row_chains):
    i = pl.program_id(0)
    tm = x_ref.shape[0]
    halo = xp_ref.shape[1]

    @pl.when(pl.program_id(1) == 0)
    def _():
        keep = (i % tiles_per_seq != 0).astype(F32)
        h_ref[:halo, :] = _rms(xp_ref[0] * keep, g_ref[...]).astype(BF16)
        h_ref[halo:, :] = _rms(x_ref[...], g_ref[...]).astype(BF16)
        o_ref[...] = x_ref[...]

    def conv(u_ref, nrows, cw_ref, cb_ref):
        cw = cw_ref[...]
        return (cw[2:3, :] * u_ref[halo:halo + nrows, :] + cw[1:2, :] * u_ref[halo - 1:halo - 1 + nrows, :]
                + cw[0:1, :] * u_ref[halo - 2:halo - 2 + nrows, :] + cb_ref[...])

    def rows(c, r0, nrows):
        h = h_ref[r0:r0 + halo + nrows, :]
        ug_ref[c] = jnp.dot(h, wg_ref[...], preferred_element_type=F32)
        uv_ref[c] = jnp.dot(h, wv_ref[...], preferred_element_type=F32)
        yield
        gate = conv(ug_ref.at[c], nrows, cwg_ref, cbg_ref)
        val = conv(uv_ref.at[c], nrows, cwv_ref, cbv_ref)
        act = (gate * (1.0 / (1.0 + jnp.exp(-gate))) * val).astype(BF16)
        yield
        o_ref[r0:r0 + nrows, :] += jnp.dot(act, wd_ref[...], preferred_element_type=F32)

    nrows = tm // row_chains
    _run_skewed([rows(c, c * nrows, nrows) for c in range(row_chains)])


def _conv_ffn(x2, gain, w_up, conv_w, conv_b, w_down, *, layer, seq):
    n, d = x2.shape
    f = w_down.shape[1]
    tm = _tile(seq, 1024)
    tn = _tile(f, 512)
    halo = SUBLANES
    fb = f // tn
    xp = x2.reshape(n // halo, halo, d)
    row_chains = FFN_ROW_CHAINS if tm % (FFN_ROW_CHAINS * SUBLANES) == 0 else 1
    kern = functools.partial(_conv_ffn_kernel, tiles_per_seq=seq // tm, row_chains=row_chains)
    return pl.pallas_call(
        kern,
        grid=(n // tm, fb),
        in_specs=[
            pl.BlockSpec((tm, d), lambda i, j: (i, 0), pipeline_mode=pl.Buffered(1)),
            pl.BlockSpec((1, halo, d), lambda i, j: (jnp.maximum(i * (tm // halo) - 1, 0), 0, 0)),
            pl.BlockSpec((1, d), lambda i, j: (0, 0)),
            pl.BlockSpec((None, d, tn), lambda i, j: (layer, 0, j)),
            pl.BlockSpec((None, d, tn), lambda i, j: (layer, 0, j + fb)),
            pl.BlockSpec((conv_w.shape[0], tn), lambda i, j: (0, j)),
            pl.BlockSpec((conv_w.shape[0], tn), lambda i, j: (0, j + fb)),
            pl.BlockSpec((1, tn), lambda i, j: (0, j)),
            pl.BlockSpec((1, tn), lambda i, j: (0, j + fb)),
            pl.BlockSpec((None, tn, d), lambda i, j: (layer, j, 0)),
        ],
        out_specs=pl.BlockSpec((tm, d), lambda i, j: (i, 0)),
        out_shape=jax.ShapeDtypeStruct((n, d), F32),
        scratch_shapes=[pltpu.VMEM((halo + tm, d), BF16),
                        pltpu.VMEM((row_chains, halo + tm // row_chains, tn), F32),
                        pltpu.VMEM((row_chains, halo + tm // row_chains, tn), F32)],
        compiler_params=_params("parallel", "arbitrary"),
        name="conv_ffn",
    )(x2, xp, gain, w_up, w_up, conv_w, conv_w, conv_b, conv_b, w_down)


def kernel(x, attn_norm, w_in, b_forget, q_norm, k_norm, out_norm_fox, out_norm_sb, w_out, ffn_norm,
           w_up, conv_w, conv_b, w_down):
    batch, seq, d = x.shape
    depth = w_in.shape[0]
    hf = b_forget.shape[1]
    hd = q_norm.shape[1]
    wf_ = out_norm_fox.shape[1]
    ws_ = out_norm_sb.shape[1]
    hs = ws_ // hd
    n = batch * seq
    scale = hd ** -0.5
    fox_blk = _tile(seq, FOX_K_BLOCK)
    assert w_in.shape[2] == 3 * wf_ + hf + 3 * ws_ and hf * hd == wf_ and hf <= LANES

    w_in_b = w_in.astype(BF16)
    w_fox = w_in_b[:, :, :3 * wf_]
    w_sb = w_in_b[:, :, 3 * wf_ + hf:]
    w_forget = jnp.pad(w_in_b[:, :, 3 * wf_:3 * wf_ + hf], ((0, 0), (0, 0), (0, LANES - hf)))
    b_pad = jnp.pad(b_forget, ((0, 0), (0, LANES - hf)))
    ones_f = jnp.ones((depth, wf_), F32)
    ones_s = jnp.ones((depth, ws_), F32)
    colgain = jnp.concatenate([jnp.tile(q_norm * scale, (1, hf)), jnp.tile(k_norm, (1, hf)), ones_f,
                               ones_s * scale, ones_s, ones_s], axis=-1)
    w_out_b = w_out.astype(BF16)
    w_up_b = w_up.astype(BF16)
    w_down_b = w_down.astype(BF16)

    x2 = x.reshape(n, d)
    for l in range(depth):
        proj, f = _in_proj(x2, attn_norm[l][None], w_fox, w_sb, colgain[l][None], w_forget, layer=l,
                           norm_cols=2 * wf_, head_dim=hd)
        c, qa, ka = _forget_cum(f.reshape(batch, seq, LANES), b_pad[l][None], n_heads=hf, head_dim=hd)
        c_heads = jnp.swapaxes(c[:, :, :hf], 1, 2).reshape(batch * hf, seq)
        c_edge = c_heads[:, ::fox_blk]
        qk_bound = (1.01 * hd * scale * jnp.max(jnp.abs(q_norm[l])) * jnp.max(jnp.abs(k_norm[l]))).reshape(1)
        proj3 = proj.reshape(batch, seq, proj.shape[1])
        out_a = _fox_attn(proj3, qa, ka, c_heads[:, None, :], c_edge, qk_bound, n_heads=hf, head_dim=hd,
                          q_off=0, k_off=hf, v_off=2 * hf, out_width=wf_)
        out_b = _sb_attn(proj3, n_heads=hs, head_dim=hd, q_off=3 * hf, k_off=3 * hf + hs,
                         v_off=3 * hf + 2 * hs, out_width=ws_)
        x2 = _out_proj(out_a.reshape(n, wf_), out_b.reshape(n, ws_), out_norm_fox[l][None],
                       out_norm_sb[l][None], w_out_b, x2, layer=l)
        x2 = _conv_ffn(x2, ffn_norm[l][None], w_up_b, conv_w[l], conv_b[l][None], w_down_b, layer=l, seq=seq)
    return x2.reshape(batch, seq, d)
```

```python
import functools
import math

import jax
import jax.numpy as jnp
from jax import lax
from jax.experimental import pallas as pl
from jax.experimental.pallas import tpu as pltpu

EPS = 1e-6
LANES = 128
SUBLANES = 8
MXU_DIM = 256
VMEM_LIMIT_BYTES = 60 * 1024 * 1024
MASK_VALUE = -1e30
EXP_UNDERFLOW = 90.0
FOX_Q_BLOCK = 512
FOX_K_BLOCK = 512
FOX_HEADS_PER_STEP = 2
SB_HEADS_PER_STEP = 8
FFN_ROW_CHAINS = 2
OUT_ROW_CHAINS = 2
IN_ROW_CHAINS = 2

F32 = jnp.float32
BF16 = jnp.bfloat16
_NT = (((1,), (1,)), ((), ()))
_TN = (((0,), (0,)), ((), ()))


def _tile(dim, want):
    t = min(dim, want)
    while dim % t:
        t -= 1
    return t


def _heads_per_step(want, n_heads, *col_offsets):
    g = want
    while g > 1 and any(v % g for v in (n_heads,) + col_offsets):
        g //= 2
    return g


def _run_skewed(chains):
    results = [None] * len(chains)
    started = 0
    live = []
    while started < len(chains) or live:
        if started < len(chains):
            live.append(started)
            started += 1
        for c in list(live):
            try:
                next(chains[c])
            except StopIteration as done:
                results[c] = done.value
                live.remove(c)
    return results


def _params(*sem):
    return pltpu.CompilerParams(dimension_semantics=sem, vmem_limit_bytes=VMEM_LIMIT_BYTES)


def _rms(x, gain):
    ms = jnp.mean(x * x, axis=-1, keepdims=True)
    return x * lax.rsqrt(ms + EPS) * gain


def _in_proj_kernel(x_ref, g_ref, wa_ref, wb_ref, cg_ref, wf_ref, o_ref, f_ref, h_ref, *,
                    norm_tiles, a_tiles, head_dim, row_chains):
    j = pl.program_id(1)
    tm, tn = o_ref.shape

    @pl.when(j == 0)
    def _():
        h = _rms(x_ref[...], g_ref[...]).astype(BF16)
        h_ref[...] = h
        f_ref[...] = jnp.dot(h, wf_ref[...], preferred_element_type=F32)

    def rows(w_ref, normed, r0, nrows):
        rsl = slice(r0, r0 + nrows)
        acc = jnp.dot(h_ref[rsl, :], w_ref[...], preferred_element_type=F32)
        yield
        if normed:
            for g in range(tn // head_dim):
                sl = slice(g * head_dim, (g + 1) * head_dim)
                o_ref[rsl, sl] = _rms(acc[:, sl], cg_ref[:, sl]).astype(o_ref.dtype)
        else:
            o_ref[rsl, :] = (acc * cg_ref[...]).astype(o_ref.dtype)

    def tile(w_ref, normed):
        nrows = tm // row_chains
        _run_skewed([rows(w_ref, normed, c * nrows, nrows) for c in range(row_chains)])

    pl.when(j < norm_tiles)(lambda: tile(wa_ref, True))
    pl.when(jnp.logical_and(j >= norm_tiles, j < a_tiles))(lambda: tile(wa_ref, False))
    pl.when(j >= a_tiles)(lambda: tile(wb_ref, False))


def _in_proj(x2, gain, wa, wb, colgain, wf, *, layer, norm_cols, head_dim):
    n, d = x2.shape
    ca, cb = wa.shape[2], wb.shape[2]
    tm = _tile(n, 1024)
    tn = _tile(math.gcd(norm_cols, ca, cb), 1024)
    assert tn % head_dim == 0
    a_tiles = ca // tn
    row_chains = IN_ROW_CHAINS if tm % (IN_ROW_CHAINS * 2 * SUBLANES) == 0 else 1
    kern = functools.partial(_in_proj_kernel, norm_tiles=norm_cols // tn, a_tiles=a_tiles, head_dim=head_dim,
                             row_chains=row_chains)
    return pl.pallas_call(
        kern,
        grid=(n // tm, (ca + cb) // tn),
        in_specs=[
            pl.BlockSpec((tm, d), lambda i, j: (i, 0)),
            pl.BlockSpec((1, d), lambda i, j: (0, 0)),
            pl.BlockSpec((None, d, tn), lambda i, j: (layer, 0, jnp.minimum(j, a_tiles - 1))),
            pl.BlockSpec((None, d, tn), lambda i, j: (layer, 0, jnp.maximum(j - a_tiles, 0))),
            pl.BlockSpec((1, tn), lambda i, j: (0, j)),
            pl.BlockSpec((None, d, LANES), lambda i, j: (layer, 0, 0)),
        ],
        out_specs=[
            pl.BlockSpec((tm, tn), lambda i, j: (i, j)),
            pl.BlockSpec((tm, LANES), lambda i, j: (i, 0)),
        ],
        out_shape=[
            jax.ShapeDtypeStruct((n, ca + cb), BF16),
            jax.ShapeDtypeStruct((n, LANES), F32),
        ],
        scratch_shapes=[pltpu.VMEM((tm, d), BF16)],
        compiler_params=_params("parallel", "arbitrary"),
        name="in_proj",
    )(x2, gain, wa, wb, colgain, wf)


def _forget_cum_kernel(f_ref, b_ref, c_ref, qa_ref, ka_ref, carry_ref, *, n_heads, head_dim):
    @pl.when(pl.program_id(1) == 0)
    def _():
        carry_ref[...] = jnp.zeros_like(carry_ref)

    z = f_ref[0] + b_ref[...]
    v = jnp.minimum(z, 0.0) - jnp.log1p(jnp.exp(-jnp.abs(z)))
    rows = v.shape[0]
    row = lax.broadcasted_iota(jnp.int32, v.shape, 0)
    shift = 1
    while shift < rows:
        v = v + jnp.where(row >= shift, pltpu.roll(v, shift, axis=0), 0.0)
        shift *= 2
    v = v + carry_ref[...]
    c_ref[0] = v
    carry_ref[...] = v[rows - 1:rows, :]

    lane = lax.broadcasted_iota(jnp.int32, (rows, head_dim), 1)
    for h in range(n_heads):
        c = jnp.broadcast_to(v[:, h:h + 1], (rows, head_dim))
        c1 = c.astype(BF16).astype(F32)
        c2 = (c - c1).astype(BF16).astype(F32)
        c3 = c - c1 - c2
        ones = jnp.where(lane < 6, 1.0, 0.0)
        q_cols = jnp.where(lane == 0, c1, jnp.where(lane == 1, c2, jnp.where(lane == 2, c3, ones)))
        k_cols = jnp.where(lane == 3, -c1, jnp.where(lane == 4, -c2, jnp.where(lane == 5, -c3, ones)))
        sl = slice(h * head_dim, (h + 1) * head_dim)
        qa_ref[0, :, sl] = q_cols.astype(BF16)
        ka_ref[0, :, sl] = k_cols.astype(BF16)


def _forget_cum(f3, bias, *, n_heads, head_dim):
    b, s, _ = f3.shape
    tc = _tile(s, 1024)
    wide = n_heads * head_dim
    kern = functools.partial(_forget_cum_kernel, n_heads=n_heads, head_dim=head_dim)
    return pl.pallas_call(
        kern,
        grid=(b, s // tc),
        in_specs=[
            pl.BlockSpec((1, tc, LANES), lambda i, t: (i, t, 0)),
            pl.BlockSpec((1, LANES), lambda i, t: (0, 0)),
        ],
        out_specs=[
            pl.BlockSpec((1, tc, LANES), lambda i, t: (i, t, 0)),
            pl.BlockSpec((1, tc, wide), lambda i, t: (i, t, 0)),
            pl.BlockSpec((1, tc, wide), lambda i, t: (i, t, 0)),
        ],
        out_shape=[
            jax.ShapeDtypeStruct(f3.shape, F32),
            jax.ShapeDtypeStruct((b, s, wide), BF16),
            jax.ShapeDtypeStruct((b, s, wide), BF16),
        ],
        scratch_shapes=[pltpu.VMEM((1, LANES), F32)],
        compiler_params=_params("parallel", "arbitrary"),
        name="forget_cum",
    )(f3, bias)


def _fox_kernel(cedge_ref, qk_bound_ref, q_ref, qa_ref, k_ref, ka_ref, v_ref, cq_ref, o_ref, *,
                blk, kblk, hd, heads):
    bi, g, qi = pl.program_id(0), pl.program_id(1), pl.program_id(2)
    first_head = (bi * pl.num_programs(1) + g) * heads
    cols = [slice(h * hd, (h + 1) * hd) for h in range(heads)]
    qs = [jnp.concatenate([q_ref[0, :, c], qa_ref[0, :, c]], axis=1) for c in cols]

    def blocks(h, kbs, state):
        m, l, acc = state
        for kb, diag in kbs:
            start = pl.multiple_of(kb * kblk, kblk)
            k = jnp.concatenate([k_ref[0, pl.ds(start, kblk), cols[h]], ka_ref[0, pl.ds(start, kblk), cols[h]]],
                                axis=1)
            s = lax.dot_general(k, qs[h], _NT, preferred_element_type=F32)
            yield
            if diag is not None:
                s = jnp.where(lax.broadcasted_iota(jnp.int32, s.shape, 0) + diag
                              <= lax.broadcasted_iota(jnp.int32, s.shape, 1), s, MASK_VALUE)
            m_new = jnp.maximum(m, jnp.max(s, axis=0, keepdims=True))
            alpha = jnp.exp(m - m_new)
            p = jnp.exp(s - m_new)
            l = alpha * l + jnp.sum(p, axis=0, keepdims=True)
            p_b = p.astype(BF16)
            m = m_new
            yield
            v = v_ref[0, pl.ds(start, kblk), cols[h]]
            acc = alpha * acc + lax.dot_general(v, p_b, _TN, preferred_element_type=F32)
        return m, l, acc

    def more_left(kb, ms):
        slack = [jnp.max(cq_ref[h] - ms[h]) - cedge_ref[first_head + h, kb] for h in range(heads)]
        return functools.reduce(jnp.maximum, slack) + qk_bound_ref[0] > -EXP_UNDERFLOW

    per_q = blk // kblk
    first_kb = qi * per_q
    zeros = jnp.zeros((1, blk), F32)
    init = (zeros + MASK_VALUE, zeros, jnp.zeros((hd, blk), F32))
    state = _run_skewed([blocks(h, [(first_kb + j, j * kblk) for j in range(per_q)], init) for h in range(heads)])
    ms, ls, accs = (tuple(s[i] for s in state) for i in range(3))

    def cond(c):
        t, go = c[0], c[1]
        return jnp.logical_and(t <= first_kb, go)

    def body(c):
        t, _, ms, ls, accs = c
        kb = first_kb - t
        go = more_left(kb, ms)
        state = _run_skewed([blocks(h, [(kb, None)], (ms[h], ls[h], accs[h])) for h in range(heads)])
        ms, ls, accs = (tuple(s[i] for s in state) for i in range(3))
        return t + 1, go, ms, ls, accs

    _, _, _, ls, accs = lax.while_loop(cond, body, (jnp.int32(1), more_left(first_kb, ms), ms, ls, accs))
    for h in range(heads):
        o_ref[0, :, cols[h]] = jnp.transpose(accs[h] / ls[h])


def _fox_attn(proj3, qa3, ka3, c_row, cedge, qk_bound, *, n_heads, head_dim, q_off, k_off, v_off, out_width):
    b, s, _ = proj3.shape
    kblk = s // cedge.shape[1]
    blk = _tile(s, FOX_Q_BLOCK)
    heads = _heads_per_step(FOX_HEADS_PER_STEP, n_heads, q_off, k_off, v_off)
    wide = heads * head_dim
    groups = n_heads // heads
    kern = functools.partial(_fox_kernel, blk=blk, kblk=kblk, hd=head_dim, heads=heads)
    grid_spec = pltpu.PrefetchScalarGridSpec(
        num_scalar_prefetch=2,
        grid=(b, groups, s // blk),
        in_specs=[
            pl.BlockSpec((1, blk, wide), lambda bi, g, qi, *_: (bi, qi, q_off // heads + g)),
            pl.BlockSpec((1, blk, wide), lambda bi, g, qi, *_: (bi, qi, g)),
            pl.BlockSpec((1, s, wide), lambda bi, g, qi, *_: (bi, 0, k_off // heads + g)),
            pl.BlockSpec((1, s, wide), lambda bi, g, qi, *_: (bi, 0, g)),
            pl.BlockSpec((1, s, wide), lambda bi, g, qi, *_: (bi, 0, v_off // heads + g)),
            pl.BlockSpec((heads, 1, blk), lambda bi, g, qi, *_: (bi * groups + g, 0, qi)),
        ],
        out_specs=pl.BlockSpec((1, blk, wide), lambda bi, g, qi, *_: (bi, qi, g)),
    )
    return pl.pallas_call(
        kern,
        grid_spec=grid_spec,
        out_shape=jax.ShapeDtypeStruct((b, s, out_width), F32),
        compiler_params=_params("parallel", "parallel", "arbitrary"),
        name="fox_attn",
    )(cedge, qk_bound, proj3, qa3, proj3, ka3, proj3, c_row)


def _sb_kernel(q_ref, k_ref, v_ref, o_ref, *, blk, hd, heads):
    qi = pl.program_id(2)
    tri = (lax.broadcasted_iota(jnp.int32, (blk, blk), 0)
           > lax.broadcasted_iota(jnp.int32, (blk, blk), 1)).astype(BF16)
    qs = [q_ref[0, :, h * hd:(h + 1) * hd] for h in range(heads)]

    def block(h, kb, masked):
        start = pl.multiple_of(kb * blk, blk)
        k = k_ref[0, pl.ds(start, blk), h * hd:(h + 1) * hd]
        z = lax.dot_general(qs[h], k, _NT, preferred_element_type=F32)
        yield
        sp = jnp.maximum(z, 0.0) + jnp.log(1.0 + jnp.exp(-jnp.abs(z)))
        log_beta = z - sp
        if masked:
            allowed = (lax.broadcasted_iota(jnp.int32, z.shape, 1)
                       < lax.broadcasted_iota(jnp.int32, z.shape, 0))
            sp = jnp.where(allowed, sp, 0.0)
        sp_sum = jnp.sum(sp, axis=-1, keepdims=True)
        sp_b = sp.astype(BF16)
        yield
        inner = jnp.dot(sp_b, tri, preferred_element_type=F32)
        yield
        w = jnp.exp(log_beta - inner)
        if masked:
            w = jnp.where(allowed, w, 0.0)
        w_b = w.astype(BF16)
        yield
        v = v_ref[0, pl.ds(start, blk), h * hd:(h + 1) * hd]
        return sp_sum, jnp.dot(w_b, v, preferred_element_type=F32)

    def alive(rights):
        return functools.reduce(jnp.maximum, [jnp.max(jnp.exp(-r)) for r in rights]) > 0.0

    has_left = (qi > 0).astype(F32)
    left = jnp.maximum(qi - 1, 0)
    first = _run_skewed([blk_ for h in range(heads) for blk_ in (block(h, qi, True), block(h, left, False))])
    rights, accs = [], []
    for h in range(heads):
        (sum_d, pv_d), (sum_l, pv_l) = first[2 * h], first[2 * h + 1]
        accs.append(pv_d + (has_left * jnp.exp(-sum_d)) * pv_l)
        rights.append(sum_d + sum_l)
    rights, accs = tuple(rights), tuple(accs)

    def cond(c):
        t, go, _, _ = c
        return jnp.logical_and(t < qi, go)

    def body(c):
        t, _, rights, accs = c
        new = _run_skewed([block(h, qi - 1 - t, False) for h in range(heads)])
        accs = tuple(accs[h] + jnp.exp(-rights[h]) * new[h][1] for h in range(heads))
        rights = tuple(rights[h] + new[h][0] for h in range(heads))
        return t + 1, alive(rights), rights, accs

    _, _, _, accs = lax.while_loop(cond, body, (jnp.int32(1), alive(rights), rights, accs))
    for h in range(heads):
        o_ref[0, :, h * hd:(h + 1) * hd] = accs[h]


def _sb_attn(proj3, *, n_heads, head_dim, q_off, k_off, v_off, out_width):
    b, s, _ = proj3.shape
    blk = _tile(s, MXU_DIM)
    heads = _heads_per_step(SB_HEADS_PER_STEP, n_heads, q_off, k_off, v_off)
    wide = heads * head_dim
    kern = functools.partial(_sb_kernel, blk=blk, hd=head_dim, heads=heads)
    return pl.pallas_call(
        kern,
        grid=(b, n_heads // heads, s // blk),
        in_specs=[
            pl.BlockSpec((1, blk, wide), lambda bi, g, qi: (bi, qi, q_off // heads + g)),
            pl.BlockSpec((1, s, wide), lambda bi, g, qi: (bi, 0, k_off // heads + g), pipeline_mode=pl.Buffered(1)),
            pl.BlockSpec((1, s, wide), lambda bi, g, qi: (bi, 0, v_off // heads + g), pipeline_mode=pl.Buffered(1)),
        ],
        out_specs=pl.BlockSpec((1, blk, wide), lambda bi, g, qi: (bi, qi, g)),
        out_shape=jax.ShapeDtypeStruct((b, s, out_width), F32),
        compiler_params=_params("parallel", "parallel", "arbitrary"),
        name="sb_attn",
    )(proj3, proj3, proj3)


def _out_proj_kernel(oa_ref, ob_ref, ga_ref, gb_ref, w_ref, x_ref, o_ref, *, row_chains):
    def rows(r0, nrows):
        sl = slice(r0, r0 + nrows)
        merged = jnp.concatenate([_rms(oa_ref[sl, :], ga_ref[...]).astype(BF16),
                                  _rms(ob_ref[sl, :], gb_ref[...]).astype(BF16)], axis=1)
        yield
        o_ref[sl, :] = x_ref[sl, :] + jnp.dot(merged, w_ref[...], preferred_element_type=F32)

    nrows = o_ref.shape[0] // row_chains
    _run_skewed([rows(c * nrows, nrows) for c in range(row_chains)])


def _out_proj(oa, ob, ga, gb, w, x2, *, layer):
    n, d = x2.shape
    wa, wb = oa.shape[1], ob.shape[1]
    tm = _tile(n, 512)
    row_chains = OUT_ROW_CHAINS if tm % (OUT_ROW_CHAINS * SUBLANES) == 0 else 1
    return pl.pallas_call(
        functools.partial(_out_proj_kernel, row_chains=row_chains),
        grid=(n // tm,),
        in_specs=[
            pl.BlockSpec((tm, wa), lambda i: (i, 0)),
            pl.BlockSpec((tm, wb), lambda i: (i, 0)),
            pl.BlockSpec((1, wa), lambda i: (0, 0)),
            pl.BlockSpec((1, wb), lambda i: (0, 0)),
            pl.BlockSpec((None, wa + wb, d), lambda i: (layer, 0, 0)),
            pl.BlockSpec((tm, d), lambda i: (i, 0)),
        ],
        out_specs=pl.BlockSpec((tm, d), lambda i: (i, 0)),
        out_shape=jax.ShapeDtypeStruct((n, d), F32),
        compiler_params=_params("parallel"),
        name="out_proj",
    )(oa, ob, ga, gb, w, x2)


def _conv_ffn_kernel(x_ref, xp_ref, g_ref, wg_ref, wv_ref, cwg_ref, cwv_ref, cbg_ref, cbv_ref, wd_ref,
                     o_ref, h_ref, ug_ref, uv_ref, *, tiles_per_seq, row_chains):
    i = pl.program_id(0)
    tm = x_ref.shape[0]
    halo = xp_ref.shape[1]

    @pl.when(pl.program_id(1) == 0)
    def _():
        keep = (i % tiles_per_seq != 0).astype(F32)
        h_ref[:halo, :] = _rms(xp_ref[0] * keep, g_ref[...]).astype(BF16)
        h_ref[halo:, :] = _rms(x_ref[...], g_ref[...]).astype(BF16)
        o_ref[...] = x_ref[...]

    def conv(u_ref, nrows, cw_ref, cb_ref):
        cw = cw_ref[...]
        return (cw[2:3, :] * u_ref[halo:halo + nrows, :] + cw[1:2, :] * u_ref[halo - 1:halo - 1 + nrows, :]
                + cw[0:1, :] * u_ref[halo - 2:halo - 2 + nrows, :] + cb_ref[...])

    def rows(c, r0, nrows):
        h = h_ref[r0:r0 + halo + nrows, :]
        ug_ref[c] = jnp.dot(h, wg_ref[...], preferred_element_type=F32)
        uv_ref[c] = jnp.dot(h, wv_ref[...], preferred_element_type=F32)
        yield
        gate = conv(ug_ref.at[c], nrows, cwg_ref, cbg_ref)
        val = conv(uv_ref.at[c], nrows, cwv_ref, cbv_ref)
        act = (gate * (1.0 / (1.0 + jnp.exp(-gate))) * val).astype(BF16)
        yield
        o_ref[r0:r0 + nrows, :] += jnp.dot(act, wd_ref[...], preferred_element_type=F32)

    nrows = tm // row_chains
    _run_skewed([rows(c, c * nrows, nrows) for c in range(row_chains)])


def _conv_ffn(x2, gain, w_up, conv_w, conv_b, w_down, *, layer, seq):
    n, d = x2.shape
    f = w_down.shape[1]
    tm = _tile(seq, 1024)
    tn = _tile(f, 512)
    halo = SUBLANES
    fb = f // tn
    xp = x2.reshape(n // halo, halo, d)
    row_chains = FFN_ROW_CHAINS if tm % (FFN_ROW_CHAINS * SUBLANES) == 0 else 1
    kern = functools.partial(_conv_ffn_kernel, tiles_per_seq=seq // tm, row_chains=row_chains)
    return pl.pallas_call(
        kern,
        grid=(n // tm, fb),
        in_specs=[
            pl.BlockSpec((tm, d), lambda i, j: (i, 0), pipeline_mode=pl.Buffered(1)),
            pl.BlockSpec((1, halo, d), lambda i, j: (jnp.maximum(i * (tm // halo) - 1, 0), 0, 0)),
            pl.BlockSpec((1, d), lambda i, j: (0, 0)),
            pl.BlockSpec((None, d, tn), lambda i, j: (layer, 0, j)),
            pl.BlockSpec((None, d, tn), lambda i, j: (layer, 0, j + fb)),
            pl.BlockSpec((conv_w.shape[0], tn), lambda i, j: (0, j)),
            pl.BlockSpec((conv_w.shape[0], tn), lambda i, j: (0, j + fb)),
            pl.BlockSpec((1, tn), lambda i, j: (0, j)),
            pl.BlockSpec((1, tn), lambda i, j: (0, j + fb)),
            pl.BlockSpec((None, tn, d), lambda i, j: (layer, j, 0)),
        ],
        out_specs=pl.BlockSpec((tm, d), lambda i, j: (i, 0)),
        out_shape=jax.ShapeDtypeStruct((n, d), F32),
        scratch_shapes=[pltpu.VMEM((halo + tm, d), BF16),
                        pltpu.VMEM((row_chains, halo + tm // row_chains, tn), F32),
                        pltpu.VMEM((row_chains, halo + tm // row_chains, tn), F32)],
        compiler_params=_params("parallel", "arbitrary"),
        name="conv_ffn",
    )(x2, xp, gain, w_up, w_up, conv_w, conv_w, conv_b, conv_b, w_down)


def kernel(x, attn_norm, w_in, b_forget, q_norm, k_norm, out_norm_fox, out_norm_sb, w_out, ffn_norm,
           w_up, conv_w, conv_b, w_down):
    batch, seq, d = x.shape
    depth = w_in.shape[0]
    hf = b_forget.shape[1]
    hd = q_norm.shape[1]
    wf_ = out_norm_fox.shape[1]
    ws_ = out_norm_sb.shape[1]
    hs = ws_ // hd
    n = batch * seq
    scale = hd ** -0.5
    fox_blk = _tile(seq, FOX_K_BLOCK)
    assert w_in.shape[2] == 3 * wf_ + hf + 3 * ws_ and hf * hd == wf_ and hf <= LANES

    w_in_b = w_in.astype(BF16)
    w_fox = w_in_b[:, :, :3 * wf_]
    w_sb = w_in_b[:, :, 3 * wf_ + hf:]
    w_forget = jnp.pad(w_in_b[:, :, 3 * wf_:3 * wf_ + hf], ((0, 0), (0, 0), (0, LANES - hf)))
    b_pad = jnp.pad(b_forget, ((0, 0), (0, LANES - hf)))
    ones_f = jnp.ones((depth, wf_), F32)
    ones_s = jnp.ones((depth, ws_), F32)
    colgain = jnp.concatenate([jnp.tile(q_norm * scale, (1, hf)), jnp.tile(k_norm, (1, hf)), ones_f,
                               ones_s * scale, ones_s, ones_s], axis=-1)
    w_out_b = w_out.astype(BF16)
    w_up_b = w_up.astype(BF16)
    w_down_b = w_down.astype(BF16)

    x2 = x.reshape(n, d)
    for l in range(depth):
        proj, f = _in_proj(x2, attn_norm[l][None], w_fox, w_sb, colgain[l][None], w_forget, layer=l,
                           norm_cols=2 * wf_, head_dim=hd)
        c, qa, ka = _forget_cum(f.reshape(batch, seq, LANES), b_pad[l][None], n_heads=hf, head_dim=hd)
        c_heads = jnp.swapaxes(c[:, :, :hf], 1, 2).reshape(batch * hf, seq)
        c_edge = c_heads[:, ::fox_blk]
        qk_bound = (1.01 * hd * scale * jnp.max(jnp.abs(q_norm[l])) * jnp.max(jnp.abs(k_norm[l]))).reshape(1)
        proj3 = proj.reshape(batch, seq, proj.shape[1])
        out_a = _fox_attn(proj3, qa, ka, c_heads[:, None, :], c_edge, qk_bound, n_heads=hf, head_dim=hd,
                          q_off=0, k_off=hf, v_off=2 * hf, out_width=wf_)
        out_b = _sb_attn(proj3, n_heads=hs, head_dim=hd, q_off=3 * hf, k_off=3 * hf + hs,
                         v_off=3 * hf + 2 * hs, out_width=ws_)
        x2 = _out_proj(out_a.reshape(n, wf_), out_b.reshape(n, ws_), out_norm_fox[l][None],
                       out_norm_sb[l][None], w_out_b, x2, layer=l)
        x2 = _conv_ffn(x2, ffn_norm[l][None], w_up_b, conv_w[l], conv_b[l][None], w_down_b, layer=l, seq=seq)
    return x2.reshape(batch, seq, d)
```

```python
import functools
import math

import jax
import jax.numpy as jnp
from jax import lax
from jax.experimental import pallas as pl
from jax.experimental.pallas import tpu as pltpu

EPS = 1e-6
LANES = 128
SUBLANES = 8
MXU_DIM = 256
VMEM_LIMIT_BYTES = 60 * 1024 * 1024
MASK_VALUE = -1e30
EXP_UNDERFLOW = 90.0
FOX_Q_BLOCK = 512
FOX_K_BLOCK = 512
FOX_HEADS_PER_STEP = 2
SB_HEADS_PER_STEP = 8
FFN_ROW_CHAINS = 2
OUT_ROW_CHAINS = 2
IN_ROW_CHAINS = 2

F32 = jnp.float32
BF16 = jnp.bfloat16
_NT = (((1,), (1,)), ((), ()))
_TN = (((0,), (0,)), ((), ()))


def _tile(dim, want):
    t = min(dim, want)
    while dim % t:
        t -= 1
    return t


def _heads_per_step(want, n_heads, *col_offsets):
    g = want
    while g > 1 and any(v % g for v in (n_heads,) + col_offsets):
        g //= 2
    return g


def _run_skewed(chains):
    results = [None] * len(chains)
    started = 0
    live = []
    while started < len(chains) or live:
        if started < len(chains):
            live.append(started)
            started += 1
        for c in list(live):
            try:
                next(chains[c])
            except StopIteration as done:
                results[c] = done.value
                live.remove(c)
    return results


def _params(*sem):
    return pltpu.CompilerParams(dimension_semantics=sem, vmem_limit_bytes=VMEM_LIMIT_BYTES)


def _rms(x, gain):
    ms = jnp.mean(x * x, axis=-1, keepdims=True)
    return x * lax.rsqrt(ms + EPS) * gain


def _in_proj_kernel(x_ref, g_ref, wa_ref, wb_ref, cg_ref, wf_ref, o_ref, f_ref, h_ref, *,
                    norm_tiles, a_tiles, head_dim, row_chains):
    j = pl.program_id(1)
    tm, tn = o_ref.shape

    @pl.when(j == 0)
    def _():
        h = _rms(x_ref[...], g_ref[...]).astype(BF16)
        h_ref[...] = h
        f_ref[...] = jnp.dot(h, wf_ref[...], preferred_element_type=F32)

    def rows(w_ref, normed, r0, nrows):
        rsl = slice(r0, r0 + nrows)
        acc = jnp.dot(h_ref[rsl, :], w_ref[...], preferred_element_type=F32)
        yield
        if normed:
            for g in range(tn // head_dim):
                sl = slice(g * head_dim, (g + 1) * head_dim)
                o_ref[rsl, sl] = _rms(acc[:, sl], cg_ref[:, sl]).astype(o_ref.dtype)
        else:
            o_ref[rsl, :] = (acc * cg_ref[...]).astype(o_ref.dtype)

    def tile(w_ref, normed):
        nrows = tm // row_chains
        _run_skewed([rows(w_ref, normed, c * nrows, nrows) for c in range(row_chains)])

    pl.when(j < norm_tiles)(lambda: tile(wa_ref, True))
    pl.when(jnp.logical_and(j >= norm_tiles, j < a_tiles))(lambda: tile(wa_ref, False))
    pl.when(j >= a_tiles)(lambda: tile(wb_ref, False))


def _in_proj(x2, gain, wa, wb, colgain, wf, *, layer, norm_cols, head_dim):
    n, d = x2.shape
    ca, cb = wa.shape[2], wb.shape[2]
    tm = _tile(n, 1024)
    tn = _tile(math.gcd(norm_cols, ca, cb), 1024)
    assert tn % head_dim == 0
    a_tiles = ca // tn
    row_chains = IN_ROW_CHAINS if tm % (IN_ROW_CHAINS * 2 * SUBLANES) == 0 else 1
    kern = functools.partial(_in_proj_kernel, norm_tiles=norm_cols // tn, a_tiles=a_tiles, head_dim=head_dim,
                             row_chains=row_chains)
    return pl.pallas_call(
        kern,
        grid=(n // tm, (ca + cb) // tn),
        in_specs=[
            pl.BlockSpec((tm, d), lambda i, j: (i, 0)),
            pl.BlockSpec((1, d), lambda i, j: (0, 0)),
            pl.BlockSpec((None, d, tn), lambda i, j: (layer, 0, jnp.minimum(j, a_tiles - 1))),
            pl.BlockSpec((None, d, tn), lambda i, j: (layer, 0, jnp.maximum(j - a_tiles, 0))),
            pl.BlockSpec((1, tn), lambda i, j: (0, j)),
            pl.BlockSpec((None, d, LANES), lambda i, j: (layer, 0, 0)),
        ],
        out_specs=[
            pl.BlockSpec((tm, tn), lambda i, j: (i, j)),
            pl.BlockSpec((tm, LANES), lambda i, j: (i, 0)),
        ],
        out_shape=[
            jax.ShapeDtypeStruct((n, ca + cb), BF16),
            jax.ShapeDtypeStruct((n, LANES), F32),
        ],
        scratch_shapes=[pltpu.VMEM((tm, d), BF16)],
        compiler_params=_params("parallel", "arbitrary"),
        name="in_proj",
    )(x2, gain, wa, wb, colgain, wf)


def _forget_cum_kernel(f_ref, b_ref, c_ref, qa_ref, ka_ref, carry_ref, *, n_heads, head_dim):
    @pl.when(pl.program_id(1) == 0)
    def _():
        carry_ref[...] = jnp.zeros_like(carry_ref)

    z = f_ref[0] + b_ref[...]
    v = jnp.minimum(z, 0.0) - jnp.log1p(jnp.exp(-jnp.abs(z)))
    rows = v.shape[0]
    row = lax.broadcasted_iota(jnp.int32, v.shape, 0)
    shift = 1
    while shift < rows:
        v = v + jnp.where(row >= shift, pltpu.roll(v, shift, axis=0), 0.0)
        shift *= 2
    v = v + carry_ref[...]
    c_ref[0] = v
    carry_ref[...] = v[rows - 1:rows, :]

    lane = lax.broadcasted_iota(jnp.int32, (rows, head_dim), 1)
    for h in range(n_heads):
        c = jnp.broadcast_to(v[:, h:h + 1], (rows, head_dim))
        c1 = c.astype(BF16).astype(F32)
        c2 = (c - c1).astype(BF16).astype(F32)
        c3 = c - c1 - c2
        ones = jnp.where(lane < 6, 1.0, 0.0)
        q_cols = jnp.where(lane == 0, c1, jnp.where(lane == 1, c2, jnp.where(lane == 2, c3, ones)))
        k_cols = jnp.where(lane == 3, -c1, jnp.where(lane == 4, -c2, jnp.where(lane == 5, -c3, ones)))
        sl = slice(h * head_dim, (h + 1) * head_dim)
        qa_ref[0, :, sl] = q_cols.astype(BF16)
        ka_ref[0, :, sl] = k_cols.astype(BF16)


def _forget_cum(f3, bias, *, n_heads, head_dim):
    b, s, _ = f3.shape
    tc = _tile(s, 1024)
    wide = n_heads * head_dim
    kern = functools.partial(_forget_cum_kernel, n_heads=n_heads, head_dim=head_dim)
    return pl.pallas_call(
        kern,
        grid=(b, s // tc),
        in_specs=[
            pl.BlockSpec((1, tc, LANES), lambda i, t: (i, t, 0)),
            pl.BlockSpec((1, LANES), lambda i, t: (0, 0)),
        ],
        out_specs=[
            pl.BlockSpec((1, tc, LANES), lambda i, t: (i, t, 0)),
            pl.BlockSpec((1, tc, wide), lambda i, t: (i, t, 0)),
            pl.BlockSpec((1, tc, wide), lambda i, t: (i, t, 0)),
        ],
        out_shape=[
            jax.ShapeDtypeStruct(f3.shape, F32),
            jax.ShapeDtypeStruct((b, s, wide), BF16),
            jax.ShapeDtypeStruct((b, s, wide), BF16),
        ],
        scratch_shapes=[pltpu.VMEM((1, LANES), F32)],
        compiler_params=_params("parallel", "arbitrary"),
        name="forget_cum",
    )(f3, bias)


def _fox_kernel(cedge_ref, qk_bound_ref, q_ref, qa_ref, k_ref, ka_ref, v_ref, cq_ref, o_ref,
                m_ref, l_ref, acc_ref, *, blk, kblk, hd, heads):
    bi, g, qi = pl.program_id(0), pl.program_id(1), pl.program_id(2)
    first_head = (bi * pl.num_programs(1) + g) * heads
    cols = [slice(h * hd, (h + 1) * hd) for h in range(heads)]
    qs = [jnp.concatenate([q_ref[0, :, c], qa_ref[0, :, c]], axis=1) for c in cols]

    def blocks(h, kbs, state):
        m, l, acc = state
        for kb, diag in kbs:
            start = pl.multiple_of(kb * kblk, kblk)
            k = jnp.concatenate([k_ref[0, pl.ds(start, kblk), cols[h]], ka_ref[0, pl.ds(start, kblk), cols[h]]],
                                axis=1)
            s = lax.dot_general(k, qs[h], _NT, preferred_element_type=F32)
            yield
            if diag is not None:
                s = jnp.where(lax.broadcasted_iota(jnp.int32, s.shape, 0) + diag
                              <= lax.broadcasted_iota(jnp.int32, s.shape, 1), s, MASK_VALUE)
            m_new = jnp.maximum(m, jnp.max(s, axis=0, keepdims=True))
            alpha = jnp.exp(m - m_new)
            p = jnp.exp(s - m_new)
            l = alpha * l + jnp.sum(p, axis=0, keepdims=True)
            p_b = p.astype(BF16)
            m = m_new
            yield
            v = v_ref[0, pl.ds(start, kblk), cols[h]]
            acc = alpha * acc + lax.dot_general(v, p_b, _TN, preferred_element_type=F32)
        return m, l, acc

    def more_left(kb, ms):
        return tuple(jnp.max(cq_ref[h] - ms[h]) - cedge_ref[first_head + h, kb] + qk_bound_ref[0] > -EXP_UNDERFLOW
                     for h in range(heads))

    per_q = blk // kblk
    first_kb = qi * per_q
    zeros = jnp.zeros((1, blk), F32)
    init = (zeros + MASK_VALUE, zeros, jnp.zeros((hd, blk), F32))
    def load(h):
        return m_ref[h], l_ref[h], acc_ref[h]

    def store(h, state):
        m_ref[h], l_ref[h], acc_ref[h] = state

    first = _run_skewed([blocks(h, [(first_kb + j, j * kblk) for j in range(per_q)], init) for h in range(heads)])
    for h in range(heads):
        store(h, first[h])

    def cond(c):
        t, gos = c
        return jnp.logical_and(t <= first_kb, functools.reduce(jnp.logical_or, gos))

    def body(c):
        t, gos = c
        kb = first_kb - t
        next_gos = more_left(kb, [m_ref[h] for h in range(heads)])
        all_live = functools.reduce(jnp.logical_and, gos)

        @pl.when(all_live)
        def _():
            out = _run_skewed([blocks(h, [(kb, None)], load(h)) for h in range(heads)])
            for h in range(heads):
                store(h, out[h])

        for h in range(heads):
            @pl.when(jnp.logical_and(jnp.logical_not(all_live), gos[h]))
            def _(h=h):
                store(h, _run_skewed([blocks(h, [(kb, None)], load(h))])[0])

        return t + 1, next_gos

    lax.while_loop(cond, body, (jnp.int32(1), more_left(first_kb, [s[0] for s in first])))
    for h in range(heads):
        o_ref[0, :, cols[h]] = jnp.transpose(acc_ref[h] / l_ref[h])


def _fox_attn(proj3, qa3, ka3, c_row, cedge, qk_bound, *, n_heads, head_dim, q_off, k_off, v_off, out_width):
    b, s, _ = proj3.shape
    kblk = s // cedge.shape[1]
    blk = _tile(s, FOX_Q_BLOCK)
    heads = _heads_per_step(FOX_HEADS_PER_STEP, n_heads, q_off, k_off, v_off)
    wide = heads * head_dim
    groups = n_heads // heads
    kern = functools.partial(_fox_kernel, blk=blk, kblk=kblk, hd=head_dim, heads=heads)
    grid_spec = pltpu.PrefetchScalarGridSpec(
        num_scalar_prefetch=2,
        grid=(b, groups, s // blk),
        in_specs=[
            pl.BlockSpec((1, blk, wide), lambda bi, g, qi, *_: (bi, qi, q_off // heads + g)),
            pl.BlockSpec((1, blk, wide), lambda bi, g, qi, *_: (bi, qi, g)),
            pl.BlockSpec((1, s, wide), lambda bi, g, qi, *_: (bi, 0, k_off // heads + g)),
            pl.BlockSpec((1, s, wide), lambda bi, g, qi, *_: (bi, 0, g)),
            pl.BlockSpec((1, s, wide), lambda bi, g, qi, *_: (bi, 0, v_off // heads + g)),
            pl.BlockSpec((heads, 1, blk), lambda bi, g, qi, *_: (bi * groups + g, 0, qi)),
        ],
        out_specs=pl.BlockSpec((1, blk, wide), lambda bi, g, qi, *_: (bi, qi, g)),
        scratch_shapes=[pltpu.VMEM((heads, 1, blk), F32), pltpu.VMEM((heads, 1, blk), F32),
                        pltpu.VMEM((heads, head_dim, blk), F32)],
    )
    return pl.pallas_call(
        kern,
        grid_spec=grid_spec,
        out_shape=jax.ShapeDtypeStruct((b, s, out_width), F32),
        compiler_params=_params("parallel", "parallel", "arbitrary"),
        name="fox_attn",
    )(cedge, qk_bound, proj3, qa3, proj3, ka3, proj3, c_row)


def _sb_kernel(q_ref, k_ref, v_ref, o_ref, *, blk, hd, heads):
    qi = pl.program_id(2)
    tri = (lax.broadcasted_iota(jnp.int32, (blk, blk), 0)
           > lax.broadcasted_iota(jnp.int32, (blk, blk), 1)).astype(BF16)
    qs = [q_ref[0, :, h * hd:(h + 1) * hd] for h in range(heads)]

    def block(h, kb, masked):
        start = pl.multiple_of(kb * blk, blk)
        k = k_ref[0, pl.ds(start, blk), h * hd:(h + 1) * hd]
        z = lax.dot_general(qs[h], k, _NT, preferred_element_type=F32)
        yield
        sp = jnp.maximum(z, 0.0) + jnp.log(1.0 + jnp.exp(-jnp.abs(z)))
        log_beta = z - sp
        if masked:
            allowed = (lax.broadcasted_iota(jnp.int32, z.shape, 1)
                       < lax.broadcasted_iota(jnp.int32, z.shape, 0))
            sp = jnp.where(allowed, sp, 0.0)
        sp_sum = jnp.sum(sp, axis=-1, keepdims=True)
        sp_b = sp.astype(BF16)
        yield
        inner = jnp.dot(sp_b, tri, preferred_element_type=F32)
        yield
        w = jnp.exp(log_beta - inner)
        if masked:
            w = jnp.where(allowed, w, 0.0)
        w_b = w.astype(BF16)
        yield
        v = v_ref[0, pl.ds(start, blk), h * hd:(h + 1) * hd]
        return sp_sum, jnp.dot(w_b, v, preferred_element_type=F32)

    def alive(rights):
        return functools.reduce(jnp.maximum, [jnp.max(jnp.exp(-r)) for r in rights]) > 0.0

    has_left = (qi > 0).astype(F32)
    left = jnp.maximum(qi - 1, 0)
    first = _run_skewed([blk_ for h in range(heads) for blk_ in (block(h, qi, True), block(h, left, False))])
    rights, accs = [], []
    for h in range(heads):
        (sum_d, pv_d), (sum_l, pv_l) = first[2 * h], first[2 * h + 1]
        accs.append(pv_d + (has_left * jnp.exp(-sum_d)) * pv_l)
        rights.append(sum_d + sum_l)
    rights, accs = tuple(rights), tuple(accs)

    def cond(c):
        t, go, _, _ = c
        return jnp.logical_and(t < qi, go)

    def body(c):
        t, _, rights, accs = c
        new = _run_skewed([block(h, qi - 1 - t, False) for h in range(heads)])
        accs = tuple(accs[h] + jnp.exp(-rights[h]) * new[h][1] for h in range(heads))
        rights = tuple(rights[h] + new[h][0] for h in range(heads))
        return t + 1, alive(rights), rights, accs

    _, _, _, accs = lax.while_loop(cond, body, (jnp.int32(1), alive(rights), rights, accs))
    for h in range(heads):
        o_ref[0, :, h * hd:(h + 1) * hd] = accs[h]


def _sb_attn(proj3, *, n_heads, head_dim, q_off, k_off, v_off, out_width):
    b, s, _ = proj3.shape
    blk = _tile(s, MXU_DIM)
    heads = _heads_per_step(SB_HEADS_PER_STEP, n_heads, q_off, k_off, v_off)
    wide = heads * head_dim
    kern = functools.partial(_sb_kernel, blk=blk, hd=head_dim, heads=heads)
    return pl.pallas_call(
        kern,
        grid=(b, n_heads // heads, s // blk),
        in_specs=[
            pl.BlockSpec((1, blk, wide), lambda bi, g, qi: (bi, qi, q_off // heads + g)),
            pl.BlockSpec((1, s, wide), lambda bi, g, qi: (bi, 0, k_off // heads + g), pipeline_mode=pl.Buffered(1)),
            pl.BlockSpec((1, s, wide), lambda bi, g, qi: (bi, 0, v_off // heads + g), pipeline_mode=pl.Buffered(1)),
        ],
        out_specs=pl.BlockSpec((1, blk, wide), lambda bi, g, qi: (bi, qi, g)),
        out_shape=jax.ShapeDtypeStruct((b, s, out_width), F32),
        compiler_params=_params("parallel", "parallel", "arbitrary"),
        name="sb_attn",
    )(proj3, proj3, proj3)


def _out_proj_kernel(oa_ref, ob_ref, ga_ref, gb_ref, w_ref, x_ref, o_ref, *, row_chains):
    def rows(r0, nrows):
        sl = slice(r0, r0 + nrows)
        merged = jnp.concatenate([_rms(oa_ref[sl, :], ga_ref[...]).astype(BF16),
                                  _rms(ob_ref[sl, :], gb_ref[...]).astype(BF16)], axis=1)
        yield
        o_ref[sl, :] = x_ref[sl, :] + jnp.dot(merged, w_ref[...], preferred_element_type=F32)

    nrows = o_ref.shape[0] // row_chains
    _run_skewed([rows(c * nrows, nrows) for c in range(row_chains)])


def _out_proj(oa, ob, ga, gb, w, x2, *, layer):
    n, d = x2.shape
    wa, wb = oa.shape[1], ob.shape[1]
    tm = _tile(n, 512)
    row_chains = OUT_ROW_CHAINS if tm % (OUT_ROW_CHAINS * SUBLANES) == 0 else 1
    return pl.pallas_call(
        functools.partial(_out_proj_kernel, row_chains=row_chains),
        grid=(n // tm,),
        in_specs=[
            pl.BlockSpec((tm, wa), lambda i: (i, 0)),
            pl.BlockSpec((tm, wb), lambda i: (i, 0)),
            pl.BlockSpec((1, wa), lambda i: (0, 0)),
            pl.BlockSpec((1, wb), lambda i: (0, 0)),
            pl.BlockSpec((None, wa + wb, d), lambda i: (layer, 0, 0)),
            pl.BlockSpec((tm, d), lambda i: (i, 0)),
        ],
        out_specs=pl.BlockSpec((tm, d), lambda i: (i, 0)),
        out_shape=jax.ShapeDtypeStruct((n, d), F32),
        compiler_params=_params("parallel"),
        name="out_proj",
    )(oa, ob, ga, gb, w, x2)


def _conv_ffn_kernel(x_ref, xp_ref, g_ref, wg_ref, wv_ref, cwg_ref, cwv_ref, cbg_ref, cbv_ref, wd_ref,
                     o_ref, h_ref, ug_ref, uv_ref, *, tiles_per_seq, row_chains):
    i = pl.program_id(0)
    tm = x_ref.shape[0]
    halo = xp_ref.shape[1]

    @pl.when(pl.program_id(1) == 0)
    def _():
        keep = (i % tiles_per_seq != 0).astype(F32)
        h_ref[:halo, :] = _rms(xp_ref[0] * keep, g_ref[...]).astype(BF16)
        h_ref[halo:, :] = _rms(x_ref[...], g_ref[...]).astype(BF16)
        o_ref[...] = x_ref[...]

    def conv(u_ref, nrows, cw_ref, cb_ref):
        cw = cw_ref[...]
        return (cw[2:3, :] * u_ref[halo:halo + nrows, :] + cw[1:2, :] * u_ref[halo - 1:halo - 1 + nrows, :]
                + cw[0:1, :] * u_ref[halo - 2:halo - 2 + nrows, :] + cb_ref[...])

    def rows(c, r0, nrows):
        h = h_ref[r0:r0 + halo + nrows, :]
        ug_ref[c] = jnp.dot(h, wg_ref[...], preferred_element_type=F32)
        uv_ref[c] = jnp.dot(h, wv_ref[...], preferred_element_type=F32)
        yield
        gate = conv(ug_ref.at[c], nrows, cwg_ref, cbg_ref)
        val = conv(uv_ref.at[c], nrows, cwv_ref, cbv_ref)
        act = (gate * (1.0 / (1.0 + jnp.exp(-gate))) * val).astype(BF16)
        yield
        o_ref[r0:r0 + nrows, :] += jnp.dot(act, wd_ref[...], preferred_element_type=F32)

    nrows = tm // row_chains
    _run_skewed([rows(c, c * nrows, nrows) for c in range(row_chains)])


def _conv_ffn(x2, gain, w_up, conv_w, conv_b, w_down, *, layer, seq):
    n, d = x2.shape
    f = w_down.shape[1]
    tm = _tile(seq, 1024)
    tn = _tile(f, 512)
    halo = SUBLANES
    fb = f // tn
    xp = x2.reshape(n // halo, halo, d)
    row_chains = FFN_ROW_CHAINS if tm % (FFN_ROW_CHAINS * SUBLANES) == 0 else 1
    kern = functools.partial(_conv_ffn_kernel, tiles_per_seq=seq // tm, row_chains=row_chains)
    return pl.pallas_call(
        kern,
        grid=(n // tm, fb),
        in_specs=[
            pl.BlockSpec((tm, d), lambda i, j: (i, 0), pipeline_mode=pl.Buffered(1)),
            pl.BlockSpec((1, halo, d), lambda i, j: (jnp.maximum(i * (tm // halo) - 1, 0), 0, 0)),
            pl.BlockSpec((1, d), lambda i, j: (0, 0)),
            pl.BlockSpec((None, d, tn), lambda i, j: (layer, 0, j)),
            pl.BlockSpec((None, d, tn), lambda i, j: (layer, 0, j + fb)),
            pl.BlockSpec((conv_w.shape[0], tn), lambda i, j: (0, j)),
            pl.BlockSpec((conv_w.shape[0], tn), lambda i, j: (0, j + fb)),
            pl.BlockSpec((1, tn), lambda i, j: (0, j)),
            pl.BlockSpec((1, tn), lambda i, j: (0, j + fb)),
            pl.BlockSpec((None, tn, d), lambda i, j: (layer, j, 0)),
        ],
        out_specs=pl.BlockSpec((tm, d), lambda i, j: (i, 0)),
        out_shape=jax.ShapeDtypeStruct((n, d), F32),
        scratch_shapes=[pltpu.VMEM((halo + tm, d), BF16),
                        pltpu.VMEM((row_chains, halo + tm // row_chains, tn), F32),
                        pltpu.VMEM((row_chains, halo + tm // row_chains, tn), F32)],
        compiler_params=_params("parallel", "arbitrary"),
        name="conv_ffn",
    )(x2, xp, gain, w_up, w_up, conv_w, conv_w, conv_b, conv_b, w_down)


def kernel(x, attn_norm, w_in, b_forget, q_norm, k_norm, out_norm_fox, out_norm_sb, w_out, ffn_norm,
           w_up, conv_w, conv_b, w_down):
    batch, seq, d = x.shape
    depth = w_in.shape[0]
    hf = b_forget.shape[1]
    hd = q_norm.shape[1]
    wf_ = out_norm_fox.shape[1]
    ws_ = out_norm_sb.shape[1]
    hs = ws_ // hd
    n = batch * seq
    scale = hd ** -0.5
    fox_blk = _tile(seq, FOX_K_BLOCK)
    assert w_in.shape[2] == 3 * wf_ + hf + 3 * ws_ and hf * hd == wf_ and hf <= LANES

    w_in_b = w_in.astype(BF16)
    w_fox = w_in_b[:, :, :3 * wf_]
    w_sb = w_in_b[:, :, 3 * wf_ + hf:]
    w_forget = jnp.pad(w_in_b[:, :, 3 * wf_:3 * wf_ + hf], ((0, 0), (0, 0), (0, LANES - hf)))
    b_pad = jnp.pad(b_forget, ((0, 0), (0, LANES - hf)))
    ones_f = jnp.ones((depth, wf_), F32)
    ones_s = jnp.ones((depth, ws_), F32)
    colgain = jnp.concatenate([jnp.tile(q_norm * scale, (1, hf)), jnp.tile(k_norm, (1, hf)), ones_f,
                               ones_s * scale, ones_s, ones_s], axis=-1)
    w_out_b = w_out.astype(BF16)
    w_up_b = w_up.astype(BF16)
    w_down_b = w_down.astype(BF16)

    x2 = x.reshape(n, d)
    for l in range(depth):
        proj, f = _in_proj(x2, attn_norm[l][None], w_fox, w_sb, colgain[l][None], w_forget, layer=l,
                           norm_cols=2 * wf_, head_dim=hd)
        c, qa, ka = _forget_cum(f.reshape(batch, seq, LANES), b_pad[l][None], n_heads=hf, head_dim=hd)
        c_heads = jnp.swapaxes(c[:, :, :hf], 1, 2).reshape(batch * hf, seq)
        c_edge = c_heads[:, ::fox_blk]
        qk_bound = (1.01 * hd * scale * jnp.max(jnp.abs(q_norm[l])) * jnp.max(jnp.abs(k_norm[l]))).reshape(1)
        proj3 = proj.reshape(batch, seq, proj.shape[1])
        out_a = _fox_attn(proj3, qa, ka, c_heads[:, None, :], c_edge, qk_bound, n_heads=hf, head_dim=hd,
                          q_off=0, k_off=hf, v_off=2 * hf, out_width=wf_)
        out_b = _sb_attn(proj3, n_heads=hs, head_dim=hd, q_off=3 * hf, k_off=3 * hf + hs,
                         v_off=3 * hf + 2 * hs, out_width=ws_)
        x2 = _out_proj(out_a.reshape(n, wf_), out_b.reshape(n, ws_), out_norm_fox[l][None],
                       out_norm_sb[l][None], w_out_b, x2, layer=l)
        x2 = _conv_ffn(x2, ffn_norm[l][None], w_up_b, conv_w[l], conv_b[l][None], w_down_b, layer=l, seq=seq)
    return x2.reshape(batch, seq, d)
```

```python
import functools
import math

import jax
import jax.numpy as jnp
import numpy as np
from jax import lax
from jax.experimental import pallas as pl
from jax.experimental.pallas import tpu as pltpu

EPS = 1e-6
LANES = 128
SUBLANES = 8
MXU_DIM = 256
VMEM_LIMIT_BYTES = 60 * 1024 * 1024
MASK_VALUE = -1e30
EXP_UNDERFLOW = 90.0
FOX_Q_BLOCK = 512
FOX_K_BLOCK = 512
FOX_HEADS_PER_STEP = 2
SB_HEADS_PER_STEP = 8
FFN_ROW_CHAINS = 2
OUT_ROW_CHAINS = 2
IN_ROW_CHAINS = 2

F32 = jnp.float32
BF16 = jnp.bfloat16
_NT = (((1,), (1,)), ((), ()))
_TN = (((0,), (0,)), ((), ()))


def _tile(dim, want):
    t = min(dim, want)
    while dim % t:
        t -= 1
    return t


def _heads_per_step(want, n_heads, *col_offsets):
    g = want
    while g > 1 and any(v % g for v in (n_heads,) + col_offsets):
        g //= 2
    return g


def _run_skewed(chains):
    results = [None] * len(chains)
    started = 0
    live = []
    while started < len(chains) or live:
        if started < len(chains):
            live.append(started)
            started += 1
        for c in list(live):
            try:
                next(chains[c])
            except StopIteration as done:
                results[c] = done.value
                live.remove(c)
    return results


def _params(*sem):
    return pltpu.CompilerParams(dimension_semantics=sem, vmem_limit_bytes=VMEM_LIMIT_BYTES)


def _rms(x, gain):
    ms = jnp.mean(x * x, axis=-1, keepdims=True)
    return x * lax.rsqrt(ms + EPS) * gain


def _in_proj_kernel(x_ref, g_ref, wa_ref, wb_ref, cg_ref, wf_ref, o_ref, f_ref, h_ref, *,
                    norm_tiles, a_tiles, head_dim, row_chains):
    j = pl.program_id(1)
    tm, tn = o_ref.shape

    @pl.when(j == 0)
    def _():
        h = _rms(x_ref[...], g_ref[...]).astype(BF16)
        h_ref[...] = h
        f_ref[...] = jnp.dot(h, wf_ref[...], preferred_element_type=F32)

    def rows(w_ref, normed, r0, nrows):
        rsl = slice(r0, r0 + nrows)
        acc = jnp.dot(h_ref[rsl, :], w_ref[...], preferred_element_type=F32)
        yield
        if normed:
            for g in range(tn // head_dim):
                sl = slice(g * head_dim, (g + 1) * head_dim)
                o_ref[rsl, sl] = _rms(acc[:, sl], cg_ref[:, sl]).astype(o_ref.dtype)
        else:
            o_ref[rsl, :] = (acc * cg_ref[...]).astype(o_ref.dtype)

    def tile(w_ref, normed):
        nrows = tm // row_chains
        _run_skewed([rows(w_ref, normed, c * nrows, nrows) for c in range(row_chains)])

    pl.when(j < norm_tiles)(lambda: tile(wa_ref, True))
    pl.when(jnp.logical_and(j >= norm_tiles, j < a_tiles))(lambda: tile(wa_ref, False))
    pl.when(j >= a_tiles)(lambda: tile(wb_ref, False))


def _in_proj(x2, gain, wa, wb, colgain, wf, *, layer, norm_cols, head_dim):
    n, d = x2.shape
    ca, cb = wa.shape[2], wb.shape[2]
    tm = _tile(n, 1024)
    tn = _tile(math.gcd(norm_cols, ca, cb), 1024)
    assert tn % head_dim == 0
    a_tiles = ca // tn
    row_chains = IN_ROW_CHAINS if tm % (IN_ROW_CHAINS * 2 * SUBLANES) == 0 else 1
    kern = functools.partial(_in_proj_kernel, norm_tiles=norm_cols // tn, a_tiles=a_tiles, head_dim=head_dim,
                             row_chains=row_chains)
    return pl.pallas_call(
        kern,
        grid=(n // tm, (ca + cb) // tn),
        in_specs=[
            pl.BlockSpec((tm, d), lambda i, j: (i, 0)),
            pl.BlockSpec((1, d), lambda i, j: (0, 0)),
            pl.BlockSpec((None, d, tn), lambda i, j: (layer, 0, jnp.minimum(j, a_tiles - 1))),
            pl.BlockSpec((None, d, tn), lambda i, j: (layer, 0, jnp.maximum(j - a_tiles, 0))),
            pl.BlockSpec((1, tn), lambda i, j: (0, j)),
            pl.BlockSpec((None, d, LANES), lambda i, j: (layer, 0, 0)),
        ],
        out_specs=[
            pl.BlockSpec((tm, tn), lambda i, j: (i, j)),
            pl.BlockSpec((tm, LANES), lambda i, j: (i, 0)),
        ],
        out_shape=[
            jax.ShapeDtypeStruct((n, ca + cb), BF16),
            jax.ShapeDtypeStruct((n, LANES), F32),
        ],
        scratch_shapes=[pltpu.VMEM((tm, d), BF16)],
        compiler_params=_params("parallel", "arbitrary"),
        name="in_proj",
    )(x2, gain, wa, wb, colgain, wf)


def _forget_cum_kernel(f_ref, b_ref, selq_ref, selk_ref, c_ref, qa_ref, ka_ref, carry_ref, *, n_heads):
    @pl.when(pl.program_id(1) == 0)
    def _():
        carry_ref[...] = jnp.zeros_like(carry_ref)

    z = f_ref[0] + b_ref[...]
    v = jnp.minimum(z, 0.0) - jnp.log1p(jnp.exp(-jnp.abs(z)))
    rows = v.shape[0]
    row = lax.broadcasted_iota(jnp.int32, v.shape, 0)
    shift = 1
    while shift < rows:
        v = v + jnp.where(row >= shift, pltpu.roll(v, shift, axis=0), 0.0)
        shift *= 2
    v = v + carry_ref[...]
    c_ref[0] = v
    carry_ref[...] = v[rows - 1:rows, :]

    lane = lax.broadcasted_iota(jnp.int32, v.shape, 1)
    heads_only = lane < n_heads
    c1 = v.astype(BF16).astype(F32)
    c2 = (v - c1).astype(BF16).astype(F32)
    c3 = v - c1 - c2
    packed = (jnp.where(heads_only, c1, 0.0)
              + pltpu.roll(jnp.where(heads_only, c2, 0.0), n_heads, axis=1)
              + pltpu.roll(jnp.where(heads_only, c3, 0.0), 2 * n_heads, axis=1)
              + jnp.where(lane == 3 * n_heads, 1.0, 0.0)).astype(BF16)
    qa_ref[0] = jnp.dot(packed, selq_ref[...], preferred_element_type=F32).astype(BF16)
    ka_ref[0] = jnp.dot(packed, selk_ref[...], preferred_element_type=F32).astype(BF16)


def _forget_selectors(n_heads, head_dim):
    selq = np.zeros((LANES, n_heads * head_dim), np.float32)
    selk = np.zeros((LANES, n_heads * head_dim), np.float32)
    one = 3 * n_heads
    for h in range(n_heads):
        for term in range(3):
            selq[term * n_heads + h, h * head_dim + term] = 1.0
            selk[term * n_heads + h, h * head_dim + 3 + term] = -1.0
            selq[one, h * head_dim + 3 + term] = 1.0
            selk[one, h * head_dim + term] = 1.0
    return jnp.asarray(selq, BF16), jnp.asarray(selk, BF16)


def _forget_cum(f3, bias, *, n_heads, head_dim):
    b, s, _ = f3.shape
    tc = _tile(s, 1024)
    wide = n_heads * head_dim
    assert 3 * n_heads < LANES and head_dim >= 6
    selq, selk = _forget_selectors(n_heads, head_dim)
    kern = functools.partial(_forget_cum_kernel, n_heads=n_heads)
    return pl.pallas_call(
        kern,
        grid=(b, s // tc),
        in_specs=[
            pl.BlockSpec((1, tc, LANES), lambda i, t: (i, t, 0)),
            pl.BlockSpec((1, LANES), lambda i, t: (0, 0)),
            pl.BlockSpec((LANES, wide), lambda i, t: (0, 0)),
            pl.BlockSpec((LANES, wide), lambda i, t: (0, 0)),
        ],
        out_specs=[
            pl.BlockSpec((1, tc, LANES), lambda i, t: (i, t, 0)),
            pl.BlockSpec((1, tc, wide), lambda i, t: (i, t, 0)),
            pl.BlockSpec((1, tc, wide), lambda i, t: (i, t, 0)),
        ],
        out_shape=[
            jax.ShapeDtypeStruct(f3.shape, F32),
            jax.ShapeDtypeStruct((b, s, wide), BF16),
            jax.ShapeDtypeStruct((b, s, wide), BF16),
        ],
        scratch_shapes=[pltpu.VMEM((1, LANES), F32)],
        compiler_params=_params("parallel", "arbitrary"),
        name="forget_cum",
    )(f3, bias, selq, selk)


def _fox_kernel(cedge_ref, qk_bound_ref, q_ref, qa_ref, k_ref, ka_ref, v_ref, cq_ref, o_ref,
                m_ref, l_ref, acc_ref, go_ref, *, blk, kblk, hd, heads):
    bi, g, qi = pl.program_id(0), pl.program_id(1), pl.program_id(2)
    first_head = (bi * pl.num_programs(1) + g) * heads
    cols = [slice(h * hd, (h + 1) * hd) for h in range(heads)]
    qs = [jnp.concatenate([q_ref[0, :, c], qa_ref[0, :, c]], axis=1) for c in cols]

    def blocks(h, kbs, state):
        m, l, acc = state
        for kb, diag in kbs:
            start = pl.multiple_of(kb * kblk, kblk)
            k = jnp.concatenate([k_ref[0, pl.ds(start, kblk), cols[h]], ka_ref[0, pl.ds(start, kblk), cols[h]]],
                                axis=1)
            s = lax.dot_general(k, qs[h], _NT, preferred_element_type=F32)
            yield
            if diag is not None:
                s = jnp.where(lax.broadcasted_iota(jnp.int32, s.shape, 0) + diag
                              <= lax.broadcasted_iota(jnp.int32, s.shape, 1), s, MASK_VALUE)
            m_new = jnp.maximum(m, jnp.max(s, axis=0, keepdims=True))
            alpha = jnp.exp(m - m_new)
            p = jnp.exp(s - m_new)
            l = alpha * l + jnp.sum(p, axis=0, keepdims=True)
            p_b = p.astype(BF16)
            m = m_new
            yield
            v = v_ref[0, pl.ds(start, kblk), cols[h]]
            acc = alpha * acc + lax.dot_general(v, p_b, _TN, preferred_element_type=F32)
        return m, l, acc

    def more_left(h, kb, m):
        return (jnp.max(cq_ref[h] - m) - cedge_ref[first_head + h, kb] + qk_bound_ref[0] > -EXP_UNDERFLOW
                ).astype(jnp.int32)

    per_q = blk // kblk
    first_kb = qi * per_q
    zeros = jnp.zeros((1, blk), F32)
    init = (zeros + MASK_VALUE, zeros, jnp.zeros((hd, blk), F32))

    def load(h):
        return m_ref[h], l_ref[h], acc_ref[h]

    def store(h, state):
        m_ref[h], l_ref[h], acc_ref[h] = state

    first = _run_skewed([blocks(h, [(first_kb + j, j * kblk) for j in range(per_q)], init) for h in range(heads)])
    for h in range(heads):
        store(h, first[h])
        go_ref[h] = more_left(h, first_kb, first[h][0])

    def live():
        return tuple(go_ref[h] != 0 for h in range(heads))

    def cond(c):
        t, gos = c
        return jnp.logical_and(t <= first_kb, functools.reduce(jnp.logical_or, gos))

    def body(c):
        t, gos = c
        kb = first_kb - t
        all_live = functools.reduce(jnp.logical_and, gos)

        @pl.when(all_live)
        def _():
            state = [load(h) for h in range(heads)]
            flags = [more_left(h, kb, state[h][0]) for h in range(heads)]
            out = _run_skewed([blocks(h, [(kb, None)], state[h]) for h in range(heads)])
            for h in range(heads):
                store(h, out[h])
                go_ref[h] = flags[h]

        for h in range(heads):
            @pl.when(jnp.logical_and(jnp.logical_not(all_live), gos[h]))
            def _(h=h):
                state = load(h)
                flag = more_left(h, kb, state[0])
                store(h, _run_skewed([blocks(h, [(kb, None)], state)])[0])
                go_ref[h] = flag

        return t + 1, live()

    lax.while_loop(cond, body, (jnp.int32(1), live()))
    for h in range(heads):
        o_ref[0, :, cols[h]] = jnp.transpose(acc_ref[h] / l_ref[h])


def _fox_attn(proj3, qa3, ka3, c_row, cedge, qk_bound, *, n_heads, head_dim, q_off, k_off, v_off, out_width):
    b, s, _ = proj3.shape
    kblk = s // cedge.shape[1]
    blk = _tile(s, FOX_Q_BLOCK)
    heads = _heads_per_step(FOX_HEADS_PER_STEP, n_heads, q_off, k_off, v_off)
    wide = heads * head_dim
    groups = n_heads // heads
    kern = functools.partial(_fox_kernel, blk=blk, kblk=kblk, hd=head_dim, heads=heads)
    grid_spec = pltpu.PrefetchScalarGridSpec(
        num_scalar_prefetch=2,
        grid=(b, groups, s // blk),
        in_specs=[
            pl.BlockSpec((1, blk, wide), lambda bi, g, qi, *_: (bi, qi, q_off // heads + g)),
            pl.BlockSpec((1, blk, wide), lambda bi, g, qi, *_: (bi, qi, g)),
            pl.BlockSpec((1, s, wide), lambda bi, g, qi, *_: (bi, 0, k_off // heads + g)),
            pl.BlockSpec((1, s, wide), lambda bi, g, qi, *_: (bi, 0, g)),
            pl.BlockSpec((1, s, wide), lambda bi, g, qi, *_: (bi, 0, v_off // heads + g)),
            pl.BlockSpec((heads, 1, blk), lambda bi, g, qi, *_: (bi * groups + g, 0, qi)),
        ],
        out_specs=pl.BlockSpec((1, blk, wide), lambda bi, g, qi, *_: (bi, qi, g)),
        scratch_shapes=[pltpu.VMEM((heads, 1, blk), F32), pltpu.VMEM((heads, 1, blk), F32),
                        pltpu.VMEM((heads, head_dim, blk), F32), pltpu.SMEM((heads,), jnp.int32)],
    )
    return pl.pallas_call(
        kern,
        grid_spec=grid_spec,
        out_shape=jax.ShapeDtypeStruct((b, s, out_width), F32),
        compiler_params=_params("parallel", "parallel", "arbitrary"),
        name="fox_attn",
    )(cedge, qk_bound, proj3, qa3, proj3, ka3, proj3, c_row)


def _sb_kernel(q_ref, k_ref, v_ref, o_ref, *, blk, hd, heads):
    qi = pl.program_id(2)
    tri = (lax.broadcasted_iota(jnp.int32, (blk, blk), 0)
           > lax.broadcasted_iota(jnp.int32, (blk, blk), 1)).astype(BF16)
    qs = [q_ref[0, :, h * hd:(h + 1) * hd] for h in range(heads)]

    def block(h, kb, masked):
        start = pl.multiple_of(kb * blk, blk)
        k = k_ref[0, pl.ds(start, blk), h * hd:(h + 1) * hd]
        z = lax.dot_general(qs[h], k, _NT, preferred_element_type=F32)
        yield
        sp = jnp.maximum(z, 0.0) + jnp.log(1.0 + jnp.exp(-jnp.abs(z)))
        log_beta = z - sp
        if masked:
            allowed = (lax.broadcasted_iota(jnp.int32, z.shape, 1)
                       < lax.broadcasted_iota(jnp.int32, z.shape, 0))
            sp = jnp.where(allowed, sp, 0.0)
        sp_sum = jnp.sum(sp, axis=-1, keepdims=True)
        sp_b = sp.astype(BF16)
        yield
        inner = jnp.dot(sp_b, tri, preferred_element_type=F32)
        yield
        w = jnp.exp(log_beta - inner)
        if masked:
            w = jnp.where(allowed, w, 0.0)
        w_b = w.astype(BF16)
        yield
        v = v_ref[0, pl.ds(start, blk), h * hd:(h + 1) * hd]
        return sp_sum, jnp.dot(w_b, v, preferred_element_type=F32)

    def alive(rights):
        return functools.reduce(jnp.maximum, [jnp.max(jnp.exp(-r)) for r in rights]) > 0.0

    has_left = (qi > 0).astype(F32)
    left = jnp.maximum(qi - 1, 0)
    first = _run_skewed([blk_ for h in range(heads) for blk_ in (block(h, qi, True), block(h, left, False))])
    rights, accs = [], []
    for h in range(heads):
        (sum_d, pv_d), (sum_l, pv_l) = first[2 * h], first[2 * h + 1]
        accs.append(pv_d + (has_left * jnp.exp(-sum_d)) * pv_l)
        rights.append(sum_d + sum_l)
    rights, accs = tuple(rights), tuple(accs)

    def cond(c):
        t, go, _, _ = c
        return jnp.logical_and(t < qi, go)

    def body(c):
        t, _, rights, accs = c
        new = _run_skewed([block(h, qi - 1 - t, False) for h in range(heads)])
        accs = tuple(accs[h] + jnp.exp(-rights[h]) * new[h][1] for h in range(heads))
        rights = tuple(rights[h] + new[h][0] for h in range(heads))
        return t + 1, alive(rights), rights, accs

    _, _, _, accs = lax.while_loop(cond, body, (jnp.int32(1), alive(rights), rights, accs))
    for h in range(heads):
        o_ref[0, :, h * hd:(h + 1) * hd] = accs[h]


def _sb_attn(proj3, *, n_heads, head_dim, q_off, k_off, v_off, out_width):
    b, s, _ = proj3.shape
    blk = _tile(s, MXU_DIM)
    heads = _heads_per_step(SB_HEADS_PER_STEP, n_heads, q_off, k_off, v_off)
    wide = heads * head_dim
    kern = functools.partial(_sb_kernel, blk=blk, hd=head_dim, heads=heads)
    return pl.pallas_call(
        kern,
        grid=(b, n_heads // heads, s // blk),
        in_specs=[
            pl.BlockSpec((1, blk, wide), lambda bi, g, qi: (bi, qi, q_off // heads + g)),
            pl.BlockSpec((1, s, wide), lambda bi, g, qi: (bi, 0, k_off // heads + g), pipeline_mode=pl.Buffered(1)),
            pl.BlockSpec((1, s, wide), lambda bi, g, qi: (bi, 0, v_off // heads + g), pipeline_mode=pl.Buffered(1)),
        ],
        out_specs=pl.BlockSpec((1, blk, wide), lambda bi, g, qi: (bi, qi, g)),
        out_shape=jax.ShapeDtypeStruct((b, s, out_width), F32),
        compiler_params=_params("parallel", "parallel", "arbitrary"),
        name="sb_attn",
    )(proj3, proj3, proj3)


def _out_proj_kernel(oa_ref, ob_ref, ga_ref, gb_ref, w_ref, x_ref, o_ref, *, row_chains):
    def rows(r0, nrows):
        sl = slice(r0, r0 + nrows)
        merged = jnp.concatenate([_rms(oa_ref[sl, :], ga_ref[...]).astype(BF16),
                                  _rms(ob_ref[sl, :], gb_ref[...]).astype(BF16)], axis=1)
        yield
        o_ref[sl, :] = x_ref[sl, :] + jnp.dot(merged, w_ref[...], preferred_element_type=F32)

    nrows = o_ref.shape[0] // row_chains
    _run_skewed([rows(c * nrows, nrows) for c in range(row_chains)])


def _out_proj(oa, ob, ga, gb, w, x2, *, layer):
    n, d = x2.shape
    wa, wb = oa.shape[1], ob.shape[1]
    tm = _tile(n, 512)
    row_chains = OUT_ROW_CHAINS if tm % (OUT_ROW_CHAINS * SUBLANES) == 0 else 1
    return pl.pallas_call(
        functools.partial(_out_proj_kernel, row_chains=row_chains),
        grid=(n // tm,),
        in_specs=[
            pl.BlockSpec((tm, wa), lambda i: (i, 0)),
            pl.BlockSpec((tm, wb), lambda i: (i, 0)),
            pl.BlockSpec((1, wa), lambda i: (0, 0)),
            pl.BlockSpec((1, wb), lambda i: (0, 0)),
            pl.BlockSpec((None, wa + wb, d), lambda i: (layer, 0, 0)),
            pl.BlockSpec((tm, d), lambda i: (i, 0)),
        ],
        out_specs=pl.BlockSpec((tm, d), lambda i: (i, 0)),
        out_shape=jax.ShapeDtypeStruct((n, d), F32),
        compiler_params=_params("parallel"),
        name="out_proj",
    )(oa, ob, ga, gb, w, x2)


def _conv_ffn_kernel(x_ref, xp_ref, g_ref, wg_ref, wv_ref, cwg_ref, cwv_ref, cbg_ref, cbv_ref, wd_ref,
                     o_ref, h_ref, ug_ref, uv_ref, *, tiles_per_seq, row_chains):
    i = pl.program_id(0)
    tm = x_ref.shape[0]
    halo = xp_ref.shape[1]

    @pl.when(pl.program_id(1) == 0)
    def _():
        keep = (i % tiles_per_seq != 0).astype(F32)
        h_ref[:halo, :] = _rms(xp_ref[0] * keep, g_ref[...]).astype(BF16)
        h_ref[halo:, :] = _rms(x_ref[...], g_ref[...]).astype(BF16)
        o_ref[...] = x_ref[...]

    def conv(u_ref, nrows, cw_ref, cb_ref):
        cw = cw_ref[...]
        return (cw[2:3, :] * u_ref[halo:halo + nrows, :] + cw[1:2, :] * u_ref[halo - 1:halo - 1 + nrows, :]
                + cw[0:1, :] * u_ref[halo - 2:halo - 2 + nrows, :] + cb_ref[...])

    def rows(c, r0, nrows):
        h = h_ref[r0:r0 + halo + nrows, :]
        ug_ref[c] = jnp.dot(h, wg_ref[...], preferred_element_type=F32)
        uv_ref[c] = jnp.dot(h, wv_ref[...], preferred_element_type=F32)
        yield
        gate = conv(ug_ref.at[c], nrows, cwg_ref, cbg_ref)
        val = conv(uv_ref.at[c], nrows, cwv_ref, cbv_ref)
        act = (gate * (1.0 / (1.0 + jnp.exp(-gate))) * val).astype(BF16)
        yield
        o_ref[r0:r0 + nrows, :] += jnp.dot(act, wd_ref[...], preferred_element_type=F32)

    nrows = tm // row_chains
    _run_skewed([rows(c, c * nrows, nrows) for c in range(row_chains)])


def _conv_ffn(x2, gain, w_up, conv_w, conv_b, w_down, *, layer, seq):
    n, d = x2.shape
    f = w_down.shape[1]
    tm = _tile(seq, 1024)
    tn = _tile(f, 512)
    halo = SUBLANES
    fb = f // tn
    xp = x2.reshape(n // halo, halo, d)
    row_chains = FFN_ROW_CHAINS if tm % (FFN_ROW_CHAINS * SUBLANES) == 0 else 1
    kern = functools.partial(_conv_ffn_kernel, tiles_per_seq=seq // tm, row_chains=row_chains)
    return pl.pallas_call(
        kern,
        grid=(n // tm, fb),
        in_specs=[
            pl.BlockSpec((tm, d), lambda i, j: (i, 0), pipeline_mode=pl.Buffered(1)),
            pl.BlockSpec((1, halo, d), lambda i, j: (jnp.maximum(i * (tm // halo) - 1, 0), 0, 0)),
            pl.BlockSpec((1, d), lambda i, j: (0, 0)),
            pl.BlockSpec((None, d, tn), lambda i, j: (layer, 0, j)),
            pl.BlockSpec((None, d, tn), lambda i, j: (layer, 0, j + fb)),
            pl.BlockSpec((conv_w.shape[0], tn), lambda i, j: (0, j)),
            pl.BlockSpec((conv_w.shape[0], tn), lambda i, j: (0, j + fb)),
            pl.BlockSpec((1, tn), lambda i, j: (0, j)),
            pl.BlockSpec((1, tn), lambda i, j: (0, j + fb)),
            pl.BlockSpec((None, tn, d), lambda i, j: (layer, j, 0)),
        ],
        out_specs=pl.BlockSpec((tm, d), lambda i, j: (i, 0)),
        out_shape=jax.ShapeDtypeStruct((n, d), F32),
        scratch_shapes=[pltpu.VMEM((halo + tm, d), BF16),
                        pltpu.VMEM((row_chains, halo + tm // row_chains, tn), F32),
                        pltpu.VMEM((row_chains, halo + tm // row_chains, tn), F32)],
        compiler_params=_params("parallel", "arbitrary"),
        name="conv_ffn",
    )(x2, xp, gain, w_up, w_up, conv_w, conv_w, conv_b, conv_b, w_down)


def kernel(x, attn_norm, w_in, b_forget, q_norm, k_norm, out_norm_fox, out_norm_sb, w_out, ffn_norm,
           w_up, conv_w, conv_b, w_down):
    batch, seq, d = x.shape
    depth = w_in.shape[0]
    hf = b_forget.shape[1]
    hd = q_norm.shape[1]
    wf_ = out_norm_fox.shape[1]
    ws_ = out_norm_sb.shape[1]
    hs = ws_ // hd
    n = batch * seq
    scale = hd ** -0.5
    fox_blk = _tile(seq, FOX_K_BLOCK)
    assert w_in.shape[2] == 3 * wf_ + hf + 3 * ws_ and hf * hd == wf_ and hf <= LANES

    w_in_b = w_in.astype(BF16)
    w_fox = w_in_b[:, :, :3 * wf_]
    w_sb = w_in_b[:, :, 3 * wf_ + hf:]
    w_forget = jnp.pad(w_in_b[:, :, 3 * wf_:3 * wf_ + hf], ((0, 0), (0, 0), (0, LANES - hf)))
    b_pad = jnp.pad(b_forget, ((0, 0), (0, LANES - hf)))
    ones_f = jnp.ones((depth, wf_), F32)
    ones_s = jnp.ones((depth, ws_), F32)
    colgain = jnp.concatenate([jnp.tile(q_norm * scale, (1, hf)), jnp.tile(k_norm, (1, hf)), ones_f,
                               ones_s * scale, ones_s, ones_s], axis=-1)
    w_out_b = w_out.astype(BF16)
    w_up_b = w_up.astype(BF16)
    w_down_b = w_down.astype(BF16)

    x2 = x.reshape(n, d)
    for l in range(depth):
        proj, f = _in_proj(x2, attn_norm[l][None], w_fox, w_sb, colgain[l][None], w_forget, layer=l,
                           norm_cols=2 * wf_, head_dim=hd)
        c, qa, ka = _forget_cum(f.reshape(batch, seq, LANES), b_pad[l][None], n_heads=hf, head_dim=hd)
        c_heads = jnp.swapaxes(c[:, :, :hf], 1, 2).reshape(batch * hf, seq)
        c_edge = c_heads[:, ::fox_blk]
        qk_bound = (1.01 * hd * scale * jnp.max(jnp.abs(q_norm[l])) * jnp.max(jnp.abs(k_norm[l]))).reshape(1)
        proj3 = proj.reshape(batch, seq, proj.shape[1])
        out_a = _fox_attn(proj3, qa, ka, c_heads[:, None, :], c_edge, qk_bound, n_heads=hf, head_dim=hd,
                          q_off=0, k_off=hf, v_off=2 * hf, out_width=wf_)
        out_b = _sb_attn(proj3, n_heads=hs, head_dim=hd, q_off=3 * hf, k_off=3 * hf + hs,
                         v_off=3 * hf + 2 * hs, out_width=ws_)
        x2 = _out_proj(out_a.reshape(n, wf_), out_b.reshape(n, ws_), out_norm_fox[l][None],
                       out_norm_sb[l][None], w_out_b, x2, layer=l)
        x2 = _conv_ffn(x2, ffn_norm[l][None], w_up_b, conv_w[l], conv_b[l][None], w_down_b, layer=l, seq=seq)
    return x2.reshape(batch, seq, d)
```

```python
import functools
import math

import jax
import jax.numpy as jnp
import numpy as np
from jax import lax
from jax.experimental import pallas as pl
from jax.experimental.pallas import tpu as pltpu

EPS = 1e-6
LANES = 128
SUBLANES = 8
MXU_DIM = 256
VMEM_LIMIT_BYTES = 60 * 1024 * 1024
MASK_VALUE = -1e30
EXP_UNDERFLOW = 90.0
FOX_Q_BLOCK = 512
FOX_K_BLOCK = 512
FOX_HEADS_PER_STEP = 2
SB_HEADS_PER_STEP = 8
FFN_ROW_CHAINS = 2
OUT_ROW_CHAINS = 2
IN_ROW_CHAINS = 2

F32 = jnp.float32
BF16 = jnp.bfloat16
_NT = (((1,), (1,)), ((), ()))
_TN = (((0,), (0,)), ((), ()))


def _tile(dim, want):
    t = min(dim, want)
    while dim % t:
        t -= 1
    return t


def _heads_per_step(want, n_heads, *col_offsets):
    g = want
    while g > 1 and any(v % g for v in (n_heads,) + col_offsets):
        g //= 2
    return g


def _run_skewed(chains):
    results = [None] * len(chains)
    started = 0
    live = []
    while started < len(chains) or live:
        if started < len(chains):
            live.append(started)
            started += 1
        for c in list(live):
            try:
                next(chains[c])
            except StopIteration as done:
                results[c] = done.value
                live.remove(c)
    return results


def _params(*sem):
    return pltpu.CompilerParams(dimension_semantics=sem, vmem_limit_bytes=VMEM_LIMIT_BYTES)


def _rms(x, gain):
    ms = jnp.mean(x * x, axis=-1, keepdims=True)
    return x * lax.rsqrt(ms + EPS) * gain


def _in_proj_kernel(x_ref, g_ref, wa_ref, wb_ref, cg_ref, wf_ref, o_ref, f_ref, h_ref, *,
                    norm_tiles, a_tiles, head_dim, row_chains):
    j = pl.program_id(1)
    tm, tn = o_ref.shape

    def rows(w_ref, normed, first, r0, nrows):
        rsl = slice(r0, r0 + nrows)
        if first:
            h = _rms(x_ref[rsl, :], g_ref[...]).astype(BF16)
            h_ref[rsl, :] = h
            f_ref[rsl, :] = jnp.dot(h, wf_ref[...], preferred_element_type=F32)
            yield
        acc = jnp.dot(h_ref[rsl, :], w_ref[...], preferred_element_type=F32)
        yield
        if normed:
            for g in range(tn // head_dim):
                sl = slice(g * head_dim, (g + 1) * head_dim)
                o_ref[rsl, sl] = _rms(acc[:, sl], cg_ref[:, sl]).astype(o_ref.dtype)
        else:
            o_ref[rsl, :] = (acc * cg_ref[...]).astype(o_ref.dtype)

    def tile(w_ref, normed, first=False):
        nrows = tm // row_chains
        _run_skewed([rows(w_ref, normed, first, c * nrows, nrows) for c in range(row_chains)])

    assert norm_tiles >= 1
    pl.when(j == 0)(lambda: tile(wa_ref, True, first=True))
    pl.when(jnp.logical_and(j > 0, j < norm_tiles))(lambda: tile(wa_ref, True))
    pl.when(jnp.logical_and(j >= norm_tiles, j < a_tiles))(lambda: tile(wa_ref, False))
    pl.when(j >= a_tiles)(lambda: tile(wb_ref, False))


def _in_proj(x2, gain, wa, wb, colgain, wf, *, layer, norm_cols, head_dim):
    n, d = x2.shape
    ca, cb = wa.shape[2], wb.shape[2]
    tm = _tile(n, 1024)
    tn = _tile(math.gcd(norm_cols, ca, cb), 1024)
    assert tn % head_dim == 0
    a_tiles = ca // tn
    row_chains = IN_ROW_CHAINS if tm % (IN_ROW_CHAINS * 2 * SUBLANES) == 0 else 1
    kern = functools.partial(_in_proj_kernel, norm_tiles=norm_cols // tn, a_tiles=a_tiles, head_dim=head_dim,
                             row_chains=row_chains)
    return pl.pallas_call(
        kern,
        grid=(n // tm, (ca + cb) // tn),
        in_specs=[
            pl.BlockSpec((tm, d), lambda i, j: (i, 0)),
            pl.BlockSpec((1, d), lambda i, j: (0, 0)),
            pl.BlockSpec((None, d, tn), lambda i, j: (layer, 0, jnp.minimum(j, a_tiles - 1))),
            pl.BlockSpec((None, d, tn), lambda i, j: (layer, 0, jnp.maximum(j - a_tiles, 0))),
            pl.BlockSpec((1, tn), lambda i, j: (0, j)),
            pl.BlockSpec((None, d, LANES), lambda i, j: (layer, 0, 0)),
        ],
        out_specs=[
            pl.BlockSpec((tm, tn), lambda i, j: (i, j)),
            pl.BlockSpec((tm, LANES), lambda i, j: (i, 0)),
        ],
        out_shape=[
            jax.ShapeDtypeStruct((n, ca + cb), BF16),
            jax.ShapeDtypeStruct((n, LANES), F32),
        ],
        scratch_shapes=[pltpu.VMEM((tm, d), BF16)],
        compiler_params=_params("parallel", "arbitrary"),
        name="in_proj",
    )(x2, gain, wa, wb, colgain, wf)


def _forget_cum_kernel(f_ref, b_ref, selq_ref, selk_ref, c_ref, qa_ref, ka_ref, carry_ref, *, n_heads):
    @pl.when(pl.program_id(1) == 0)
    def _():
        carry_ref[...] = jnp.zeros_like(carry_ref)

    z = f_ref[0] + b_ref[...]
    v = jnp.minimum(z, 0.0) - jnp.log1p(jnp.exp(-jnp.abs(z)))
    rows = v.shape[0]
    row = lax.broadcasted_iota(jnp.int32, v.shape, 0)
    shift = 1
    while shift < rows:
        v = v + jnp.where(row >= shift, pltpu.roll(v, shift, axis=0), 0.0)
        shift *= 2
    v = v + carry_ref[...]
    c_ref[0] = v
    carry_ref[...] = v[rows - 1:rows, :]

    lane = lax.broadcasted_iota(jnp.int32, v.shape, 1)
    heads_only = lane < n_heads
    c1 = v.astype(BF16).astype(F32)
    c2 = (v - c1).astype(BF16).astype(F32)
    c3 = v - c1 - c2
    packed = (jnp.where(heads_only, c1, 0.0)
              + pltpu.roll(jnp.where(heads_only, c2, 0.0), n_heads, axis=1)
              + pltpu.roll(jnp.where(heads_only, c3, 0.0), 2 * n_heads, axis=1)
              + jnp.where(lane == 3 * n_heads, 1.0, 0.0)).astype(BF16)
    qa_ref[0] = jnp.dot(packed, selq_ref[...], preferred_element_type=F32).astype(BF16)
    ka_ref[0] = jnp.dot(packed, selk_ref[...], preferred_element_type=F32).astype(BF16)


def _forget_selectors(n_heads, head_dim):
    selq = np.zeros((LANES, n_heads * head_dim), np.float32)
    selk = np.zeros((LANES, n_heads * head_dim), np.float32)
    one = 3 * n_heads
    for h in range(n_heads):
        for term in range(3):
            selq[term * n_heads + h, h * head_dim + term] = 1.0
            selk[term * n_heads + h, h * head_dim + 3 + term] = -1.0
            selq[one, h * head_dim + 3 + term] = 1.0
            selk[one, h * head_dim + term] = 1.0
    return jnp.asarray(selq, BF16), jnp.asarray(selk, BF16)


def _forget_cum(f3, bias, *, n_heads, head_dim):
    b, s, _ = f3.shape
    tc = _tile(s, 1024)
    wide = n_heads * head_dim
    assert 3 * n_heads < LANES and head_dim >= 6
    selq, selk = _forget_selectors(n_heads, head_dim)
    kern = functools.partial(_forget_cum_kernel, n_heads=n_heads)
    return pl.pallas_call(
        kern,
        grid=(b, s // tc),
        in_specs=[
            pl.BlockSpec((1, tc, LANES), lambda i, t: (i, t, 0)),
            pl.BlockSpec((1, LANES), lambda i, t: (0, 0)),
            pl.BlockSpec((LANES, wide), lambda i, t: (0, 0)),
            pl.BlockSpec((LANES, wide), lambda i, t: (0, 0)),
        ],
        out_specs=[
            pl.BlockSpec((1, tc, LANES), lambda i, t: (i, t, 0)),
            pl.BlockSpec((1, tc, wide), lambda i, t: (i, t, 0)),
            pl.BlockSpec((1, tc, wide), lambda i, t: (i, t, 0)),
        ],
        out_shape=[
            jax.ShapeDtypeStruct(f3.shape, F32),
            jax.ShapeDtypeStruct((b, s, wide), BF16),
            jax.ShapeDtypeStruct((b, s, wide), BF16),
        ],
        scratch_shapes=[pltpu.VMEM((1, LANES), F32)],
        compiler_params=_params("parallel", "arbitrary"),
        name="forget_cum",
    )(f3, bias, selq, selk)


def _fox_kernel(cedge_ref, qk_bound_ref, q_ref, qa_ref, k_ref, ka_ref, v_ref, cq_ref, o_ref,
                m_ref, l_ref, acc_ref, go_ref, *, blk, kblk, hd, heads):
    bi, g, qi = pl.program_id(0), pl.program_id(1), pl.program_id(2)
    first_head = (bi * pl.num_programs(1) + g) * heads
    cols = [slice(h * hd, (h + 1) * hd) for h in range(heads)]
    qs = [jnp.concatenate([q_ref[0, :, c], qa_ref[0, :, c]], axis=1) for c in cols]

    def blocks(h, kbs, state):
        m, l, acc = state
        for kb, diag in kbs:
            start = pl.multiple_of(kb * kblk, kblk)
            k = jnp.concatenate([k_ref[0, pl.ds(start, kblk), cols[h]], ka_ref[0, pl.ds(start, kblk), cols[h]]],
                                axis=1)
            s = lax.dot_general(k, qs[h], _NT, preferred_element_type=F32)
            yield
            if diag is not None:
                s = jnp.where(lax.broadcasted_iota(jnp.int32, s.shape, 0) + diag
                              <= lax.broadcasted_iota(jnp.int32, s.shape, 1), s, MASK_VALUE)
            m_new = jnp.maximum(m, jnp.max(s, axis=0, keepdims=True))
            alpha = jnp.exp(m - m_new)
            p = jnp.exp(s - m_new)
            l = alpha * l + jnp.sum(p, axis=0, keepdims=True)
            p_b = p.astype(BF16)
            m = m_new
            yield
            v = v_ref[0, pl.ds(start, kblk), cols[h]]
            acc = alpha * acc + lax.dot_general(v, p_b, _TN, preferred_element_type=F32)
        return m, l, acc

    def more_left(h, kb, m):
        return (jnp.max(cq_ref[h] - m) - cedge_ref[first_head + h, kb] + qk_bound_ref[0] > -EXP_UNDERFLOW
                ).astype(jnp.int32)

    per_q = blk // kblk
    first_kb = qi * per_q
    zeros = jnp.zeros((1, blk), F32)
    init = (zeros + MASK_VALUE, zeros, jnp.zeros((hd, blk), F32))

    def load(h):
        return m_ref[h], l_ref[h], acc_ref[h]

    def store(h, state):
        m_ref[h], l_ref[h], acc_ref[h] = state

    first = _run_skewed([blocks(h, [(first_kb + j, j * kblk) for j in range(per_q)], init) for h in range(heads)])
    for h in range(heads):
        store(h, first[h])
        go_ref[h] = more_left(h, first_kb, first[h][0])

    def live():
        return tuple(go_ref[h] != 0 for h in range(heads))

    def cond(c):
        t, gos = c
        return jnp.logical_and(t <= first_kb, functools.reduce(jnp.logical_or, gos))

    def body(c):
        t, gos = c
        kb = first_kb - t
        all_live = functools.reduce(jnp.logical_and, gos)

        @pl.when(all_live)
        def _():
            state = [load(h) for h in range(heads)]
            flags = [more_left(h, kb, state[h][0]) for h in range(heads)]
            out = _run_skewed([blocks(h, [(kb, None)], state[h]) for h in range(heads)])
            for h in range(heads):
                store(h, out[h])
                go_ref[h] = flags[h]

        for h in range(heads):
            @pl.when(jnp.logical_and(jnp.logical_not(all_live), gos[h]))
            def _(h=h):
                state = load(h)
                flag = more_left(h, kb, state[0])
                store(h, _run_skewed([blocks(h, [(kb, None)], state)])[0])
                go_ref[h] = flag

        return t + 1, live()

    lax.while_loop(cond, body, (jnp.int32(1), live()))
    for h in range(heads):
        o_ref[0, :, cols[h]] = jnp.transpose(acc_ref[h] / l_ref[h])


def _fox_attn(proj3, qa3, ka3, c_row, cedge, qk_bound, *, n_heads, head_dim, q_off, k_off, v_off, out_width):
    b, s, _ = proj3.shape
    kblk = s // cedge.shape[1]
    blk = _tile(s, FOX_Q_BLOCK)
    heads = _heads_per_step(FOX_HEADS_PER_STEP, n_heads, q_off, k_off, v_off)
    wide = heads * head_dim
    groups = n_heads // heads
    kern = functools.partial(_fox_kernel, blk=blk, kblk=kblk, hd=head_dim, heads=heads)
    grid_spec = pltpu.PrefetchScalarGridSpec(
        num_scalar_prefetch=2,
        grid=(b, groups, s // blk),
        in_specs=[
            pl.BlockSpec((1, blk, wide), lambda bi, g, qi, *_: (bi, qi, q_off // heads + g)),
            pl.BlockSpec((1, blk, wide), lambda bi, g, qi, *_: (bi, qi, g)),
            pl.BlockSpec((1, s, wide), lambda bi, g, qi, *_: (bi, 0, k_off // heads + g)),
            pl.BlockSpec((1, s, wide), lambda bi, g, qi, *_: (bi, 0, g)),
            pl.BlockSpec((1, s, wide), lambda bi, g, qi, *_: (bi, 0, v_off // heads + g)),
            pl.BlockSpec((heads, 1, blk), lambda bi, g, qi, *_: (bi * groups + g, 0, qi)),
        ],
        out_specs=pl.BlockSpec((1, blk, wide), lambda bi, g, qi, *_: (bi, qi, g)),
        scratch_shapes=[pltpu.VMEM((heads, 1, blk), F32), pltpu.VMEM((heads, 1, blk), F32),
                        pltpu.VMEM((heads, head_dim, blk), F32), pltpu.SMEM((heads,), jnp.int32)],
    )
    return pl.pallas_call(
        kern,
        grid_spec=grid_spec,
        out_shape=jax.ShapeDtypeStruct((b, s, out_width), F32),
        compiler_params=_params("parallel", "parallel", "arbitrary"),
        name="fox_attn",
    )(cedge, qk_bound, proj3, qa3, proj3, ka3, proj3, c_row)


def _sb_kernel(q_ref, k_ref, v_ref, o_ref, *, blk, hd, heads):
    qi = pl.program_id(2)
    tri = (lax.broadcasted_iota(jnp.int32, (blk, blk), 0)
           > lax.broadcasted_iota(jnp.int32, (blk, blk), 1)).astype(BF16)
    qs = [q_ref[0, :, h * hd:(h + 1) * hd] for h in range(heads)]

    def block(h, kb, masked):
        start = pl.multiple_of(kb * blk, blk)
        k = k_ref[0, pl.ds(start, blk), h * hd:(h + 1) * hd]
        z = lax.dot_general(qs[h], k, _NT, preferred_element_type=F32)
        yield
        sp = jnp.maximum(z, 0.0) + jnp.log(1.0 + jnp.exp(-jnp.abs(z)))
        log_beta = z - sp
        if masked:
            allowed = (lax.broadcasted_iota(jnp.int32, z.shape, 1)
                       < lax.broadcasted_iota(jnp.int32, z.shape, 0))
            sp = jnp.where(allowed, sp, 0.0)
        sp_sum = jnp.sum(sp, axis=-1, keepdims=True)
        sp_b = sp.astype(BF16)
        yield
        inner = jnp.dot(sp_b, tri, preferred_element_type=F32)
        yield
        w = jnp.exp(log_beta - inner)
        if masked:
            w = jnp.where(allowed, w, 0.0)
        w_b = w.astype(BF16)
        yield
        v = v_ref[0, pl.ds(start, blk), h * hd:(h + 1) * hd]
        return sp_sum, jnp.dot(w_b, v, preferred_element_type=F32)

    def alive(rights):
        return functools.reduce(jnp.maximum, [jnp.max(jnp.exp(-r)) for r in rights]) > 0.0

    has_left = (qi > 0).astype(F32)
    left = jnp.maximum(qi - 1, 0)
    first = _run_skewed([blk_ for h in range(heads) for blk_ in (block(h, qi, True), block(h, left, False))])
    rights, accs = [], []
    for h in range(heads):
        (sum_d, pv_d), (sum_l, pv_l) = first[2 * h], first[2 * h + 1]
        accs.append(pv_d + (has_left * jnp.exp(-sum_d)) * pv_l)
        rights.append(sum_d + sum_l)
    rights, accs = tuple(rights), tuple(accs)

    def cond(c):
        t, go, _, _ = c
        return jnp.logical_and(t < qi, go)

    def body(c):
        t, _, rights, accs = c
        new = _run_skewed([block(h, qi - 1 - t, False) for h in range(heads)])
        accs = tuple(accs[h] + jnp.exp(-rights[h]) * new[h][1] for h in range(heads))
        rights = tuple(rights[h] + new[h][0] for h in range(heads))
        return t + 1, alive(rights), rights, accs

    _, _, _, accs = lax.while_loop(cond, body, (jnp.int32(1), alive(rights), rights, accs))
    for h in range(heads):
        o_ref[0, :, h * hd:(h + 1) * hd] = accs[h]


def _sb_attn(proj3, *, n_heads, head_dim, q_off, k_off, v_off, out_width):
    b, s, _ = proj3.shape
    blk = _tile(s, MXU_DIM)
    heads = _heads_per_step(SB_HEADS_PER_STEP, n_heads, q_off, k_off, v_off)
    wide = heads * head_dim
    kern = functools.partial(_sb_kernel, blk=blk, hd=head_dim, heads=heads)
    return pl.pallas_call(
        kern,
        grid=(b, n_heads // heads, s // blk),
        in_specs=[
            pl.BlockSpec((1, blk, wide), lambda bi, g, qi: (bi, qi, q_off // heads + g)),
            pl.BlockSpec((1, s, wide), lambda bi, g, qi: (bi, 0, k_off // heads + g), pipeline_mode=pl.Buffered(1)),
            pl.BlockSpec((1, s, wide), lambda bi, g, qi: (bi, 0, v_off // heads + g), pipeline_mode=pl.Buffered(1)),
        ],
        out_specs=pl.BlockSpec((1, blk, wide), lambda bi, g, qi: (bi, qi, g)),
        out_shape=jax.ShapeDtypeStruct((b, s, out_width), F32),
        compiler_params=_params("parallel", "parallel", "arbitrary"),
        name="sb_attn",
    )(proj3, proj3, proj3)


def _out_proj_kernel(oa_ref, ob_ref, ga_ref, gb_ref, w_ref, x_ref, o_ref, *, row_chains):
    def rows(r0, nrows):
        sl = slice(r0, r0 + nrows)
        merged = jnp.concatenate([_rms(oa_ref[sl, :], ga_ref[...]).astype(BF16),
                                  _rms(ob_ref[sl, :], gb_ref[...]).astype(BF16)], axis=1)
        yield
        o_ref[sl, :] = x_ref[sl, :] + jnp.dot(merged, w_ref[...], preferred_element_type=F32)

    nrows = o_ref.shape[0] // row_chains
    _run_skewed([rows(c * nrows, nrows) for c in range(row_chains)])


def _out_proj(oa, ob, ga, gb, w, x2, *, layer):
    n, d = x2.shape
    wa, wb = oa.shape[1], ob.shape[1]
    tm = _tile(n, 512)
    row_chains = OUT_ROW_CHAINS if tm % (OUT_ROW_CHAINS * SUBLANES) == 0 else 1
    return pl.pallas_call(
        functools.partial(_out_proj_kernel, row_chains=row_chains),
        grid=(n // tm,),
        in_specs=[
            pl.BlockSpec((tm, wa), lambda i: (i, 0)),
            pl.BlockSpec((tm, wb), lambda i: (i, 0)),
            pl.BlockSpec((1, wa), lambda i: (0, 0)),
            pl.BlockSpec((1, wb), lambda i: (0, 0)),
            pl.BlockSpec((None, wa + wb, d), lambda i: (layer, 0, 0)),
            pl.BlockSpec((tm, d), lambda i: (i, 0)),
        ],
        out_specs=pl.BlockSpec((tm, d), lambda i: (i, 0)),
        out_shape=jax.ShapeDtypeStruct((n, d), F32),
        compiler_params=_params("parallel"),
        name="out_proj",
    )(oa, ob, ga, gb, w, x2)


def _conv_ffn_kernel(x_ref, xp_ref, g_ref, wg_ref, wv_ref, cwg_ref, cwv_ref, cbg_ref, cbv_ref, wd_ref,
                     o_ref, h_ref, ug_ref, uv_ref, *, tiles_per_seq, row_chains):
    i = pl.program_id(0)
    tm = x_ref.shape[0]
    halo = xp_ref.shape[1]

    def conv(u_ref, nrows, cw_ref, cb_ref):
        cw = cw_ref[...]
        return (cw[2:3, :] * u_ref[halo:halo + nrows, :] + cw[1:2, :] * u_ref[halo - 1:halo - 1 + nrows, :]
                + cw[0:1, :] * u_ref[halo - 2:halo - 2 + nrows, :] + cb_ref[...])

    def rows(c, r0, nrows, first):
        if first:
            xs = x_ref[r0:r0 + nrows, :]
            h_ref[halo + r0:halo + r0 + nrows, :] = _rms(xs, g_ref[...]).astype(BF16)
            o_ref[r0:r0 + nrows, :] = xs
            yield
        h = h_ref[r0:r0 + halo + nrows, :]
        ug_ref[c] = jnp.dot(h, wg_ref[...], preferred_element_type=F32)
        uv_ref[c] = jnp.dot(h, wv_ref[...], preferred_element_type=F32)
        yield
        gate = conv(ug_ref.at[c], nrows, cwg_ref, cbg_ref)
        val = conv(uv_ref.at[c], nrows, cwv_ref, cbv_ref)
        act = (gate * (1.0 / (1.0 + jnp.exp(-gate))) * val).astype(BF16)
        yield
        o_ref[r0:r0 + nrows, :] += jnp.dot(act, wd_ref[...], preferred_element_type=F32)

    nrows = tm // row_chains
    j = pl.program_id(1)

    @pl.when(j == 0)
    def _():
        keep = (i % tiles_per_seq != 0).astype(F32)
        h_ref[:halo, :] = _rms(xp_ref[0] * keep, g_ref[...]).astype(BF16)
        _run_skewed([rows(c, c * nrows, nrows, True) for c in range(row_chains)])

    @pl.when(j > 0)
    def _():
        _run_skewed([rows(c, c * nrows, nrows, False) for c in range(row_chains)])


def _conv_ffn(x2, gain, w_up, conv_w, conv_b, w_down, *, layer, seq):
    n, d = x2.shape
    f = w_down.shape[1]
    tm = _tile(seq, 1024)
    tn = _tile(f, 512)
    halo = SUBLANES
    fb = f // tn
    xp = x2.reshape(n // halo, halo, d)
    row_chains = FFN_ROW_CHAINS if tm % (FFN_ROW_CHAINS * SUBLANES) == 0 else 1
    kern = functools.partial(_conv_ffn_kernel, tiles_per_seq=seq // tm, row_chains=row_chains)
    return pl.pallas_call(
        kern,
        grid=(n // tm, fb),
        in_specs=[
            pl.BlockSpec((tm, d), lambda i, j: (i, 0), pipeline_mode=pl.Buffered(1)),
            pl.BlockSpec((1, halo, d), lambda i, j: (jnp.maximum(i * (tm // halo) - 1, 0), 0, 0)),
            pl.BlockSpec((1, d), lambda i, j: (0, 0)),
            pl.BlockSpec((None, d, tn), lambda i, j: (layer, 0, j)),
            pl.BlockSpec((None, d, tn), lambda i, j: (layer, 0, j + fb)),
            pl.BlockSpec((conv_w.shape[0], tn), lambda i, j: (0, j)),
            pl.BlockSpec((conv_w.shape[0], tn), lambda i, j: (0, j + fb)),
            pl.BlockSpec((1, tn), lambda i, j: (0, j)),
            pl.BlockSpec((1, tn), lambda i, j: (0, j + fb)),
            pl.BlockSpec((None, tn, d), lambda i, j: (layer, j, 0)),
        ],
        out_specs=pl.BlockSpec((tm, d), lambda i, j: (i, 0)),
        out_shape=jax.ShapeDtypeStruct((n, d), F32),
        scratch_shapes=[pltpu.VMEM((halo + tm, d), BF16),
                        pltpu.VMEM((row_chains, halo + tm // row_chains, tn), F32),
                        pltpu.VMEM((row_chains, halo + tm // row_chains, tn), F32)],
        compiler_params=_params("parallel", "arbitrary"),
        name="conv_ffn",
    )(x2, xp, gain, w_up, w_up, conv_w, conv_w, conv_b, conv_b, w_down)


def kernel(x, attn_norm, w_in, b_forget, q_norm, k_norm, out_norm_fox, out_norm_sb, w_out, ffn_norm,
           w_up, conv_w, conv_b, w_down):
    batch, seq, d = x.shape
    depth = w_in.shape[0]
    hf = b_forget.shape[1]
    hd = q_norm.shape[1]
    wf_ = out_norm_fox.shape[1]
    ws_ = out_norm_sb.shape[1]
    hs = ws_ // hd
    n = batch * seq
    scale = hd ** -0.5
    fox_blk = _tile(seq, FOX_K_BLOCK)
    assert w_in.shape[2] == 3 * wf_ + hf + 3 * ws_ and hf * hd == wf_ and hf <= LANES

    w_in_b = w_in.astype(BF16)
    w_fox = w_in_b[:, :, :3 * wf_]
    w_sb = w_in_b[:, :, 3 * wf_ + hf:]
    w_forget = jnp.pad(w_in_b[:, :, 3 * wf_:3 * wf_ + hf], ((0, 0), (0, 0), (0, LANES - hf)))
    b_pad = jnp.pad(b_forget, ((0, 0), (0, LANES - hf)))
    ones_f = jnp.ones((depth, wf_), F32)
    ones_s = jnp.ones((depth, ws_), F32)
    colgain = jnp.concatenate([jnp.tile(q_norm * scale, (1, hf)), jnp.tile(k_norm, (1, hf)), ones_f,
                               ones_s * scale, ones_s, ones_s], axis=-1)
    w_out_b = w_out.astype(BF16)
    w_up_b = w_up.astype(BF16)
    w_down_b = w_down.astype(BF16)

    x2 = x.reshape(n, d)
    for l in range(depth):
        proj, f = _in_proj(x2, attn_norm[l][None], w_fox, w_sb, colgain[l][None], w_forget, layer=l,
                           norm_cols=2 * wf_, head_dim=hd)
        c, qa, ka = _forget_cum(f.reshape(batch, seq, LANES), b_pad[l][None], n_heads=hf, head_dim=hd)
        c_heads = jnp.swapaxes(c[:, :, :hf], 1, 2).reshape(batch * hf, seq)
        c_edge = c_heads[:, ::fox_blk]
        qk_bound = (1.01 * hd * scale * jnp.max(jnp.abs(q_norm[l])) * jnp.max(jnp.abs(k_norm[l]))).reshape(1)
        proj3 = proj.reshape(batch, seq, proj.shape[1])
        out_a = _fox_attn(proj3, qa, ka, c_heads[:, None, :], c_edge, qk_bound, n_heads=hf, head_dim=hd,
                          q_off=0, k_off=hf, v_off=2 * hf, out_width=wf_)
        out_b = _sb_attn(proj3, n_heads=hs, head_dim=hd, q_off=3 * hf, k_off=3 * hf + hs,
                         v_off=3 * hf + 2 * hs, out_width=ws_)
        x2 = _out_proj(out_a.reshape(n, wf_), out_b.reshape(n, ws_), out_norm_fox[l][None],
                       out_norm_sb[l][None], w_out_b, x2, layer=l)
        x2 = _conv_ffn(x2, ffn_norm[l][None], w_up_b, conv_w[l], conv_b[l][None], w_down_b, layer=l, seq=seq)
    return x2.reshape(batch, seq, d)
```

```python
import functools
import math

import jax
import jax.numpy as jnp
import numpy as np
from jax import lax
from jax.experimental import pallas as pl
from jax.experimental.pallas import tpu as pltpu

EPS = 1e-6
LANES = 128
SUBLANES = 8
MXU_DIM = 256
VMEM_LIMIT_BYTES = 60 * 1024 * 1024
MASK_VALUE = -1e30
EXP_UNDERFLOW = 90.0
FOX_Q_BLOCK = 512
FOX_K_BLOCK = 512
FOX_HEADS_PER_STEP = 2
SB_HEADS_PER_STEP = 8
FFN_ROW_CHAINS = 2
OUT_ROW_CHAINS = 2
IN_ROW_CHAINS = 2

F32 = jnp.float32
BF16 = jnp.bfloat16
_NT = (((1,), (1,)), ((), ()))
_TN = (((0,), (0,)), ((), ()))


def _tile(dim, want):
    t = min(dim, want)
    while dim % t:
        t -= 1
    return t


def _heads_per_step(want, n_heads, *col_offsets):
    g = want
    while g > 1 and any(v % g for v in (n_heads,) + col_offsets):
        g //= 2
    return g


def _run_skewed(chains):
    results = [None] * len(chains)
    started = 0
    live = []
    while started < len(chains) or live:
        if started < len(chains):
            live.append(started)
            started += 1
        for c in list(live):
            try:
                next(chains[c])
            except StopIteration as done:
                results[c] = done.value
                live.remove(c)
    return results


def _params(*sem):
    return pltpu.CompilerParams(dimension_semantics=sem, vmem_limit_bytes=VMEM_LIMIT_BYTES)


def _rms(x, gain):
    ms = jnp.mean(x * x, axis=-1, keepdims=True)
    return x * lax.rsqrt(ms + EPS) * gain


def _in_proj_kernel(x_ref, g_ref, wa_ref, wb_ref, cg_ref, wf_ref, o_ref, f_ref, h_ref, *,
                    norm_tiles, a_tiles, head_dim, row_chains):
    j = pl.program_id(1)
    tm, tn = o_ref.shape

    def rows(w_ref, normed, first, r0, nrows):
        rsl = slice(r0, r0 + nrows)
        if first:
            h = _rms(x_ref[rsl, :], g_ref[...]).astype(BF16)
            h_ref[rsl, :] = h
            f_ref[rsl, :] = jnp.dot(h, wf_ref[...], preferred_element_type=F32)
            yield
        acc = jnp.dot(h_ref[rsl, :], w_ref[...], preferred_element_type=F32)
        yield
        if normed:
            for g in range(tn // head_dim):
                sl = slice(g * head_dim, (g + 1) * head_dim)
                o_ref[rsl, sl] = _rms(acc[:, sl], cg_ref[:, sl]).astype(o_ref.dtype)
        else:
            o_ref[rsl, :] = (acc * cg_ref[...]).astype(o_ref.dtype)

    def tile(w_ref, normed, first=False):
        nrows = tm // row_chains
        _run_skewed([rows(w_ref, normed, first, c * nrows, nrows) for c in range(row_chains)])

    assert norm_tiles >= 1
    pl.when(j == 0)(lambda: tile(wa_ref, True, first=True))
    pl.when(jnp.logical_and(j > 0, j < norm_tiles))(lambda: tile(wa_ref, True))
    pl.when(jnp.logical_and(j >= norm_tiles, j < a_tiles))(lambda: tile(wa_ref, False))
    pl.when(j >= a_tiles)(lambda: tile(wb_ref, False))


def _in_proj(x2, gain, wa, wb, colgain, wf, *, layer, norm_cols, head_dim):
    n, d = x2.shape
    ca, cb = wa.shape[2], wb.shape[2]
    tm = _tile(n, 1024)
    tn = _tile(math.gcd(norm_cols, ca, cb), 1024)
    assert tn % head_dim == 0
    a_tiles = ca // tn
    row_chains = IN_ROW_CHAINS if tm % (IN_ROW_CHAINS * 2 * SUBLANES) == 0 else 1
    kern = functools.partial(_in_proj_kernel, norm_tiles=norm_cols // tn, a_tiles=a_tiles, head_dim=head_dim,
                             row_chains=row_chains)
    return pl.pallas_call(
        kern,
        grid=(n // tm, (ca + cb) // tn),
        in_specs=[
            pl.BlockSpec((tm, d), lambda i, j: (i, 0)),
            pl.BlockSpec((1, d), lambda i, j: (0, 0)),
            pl.BlockSpec((None, d, tn), lambda i, j: (layer, 0, jnp.minimum(j, a_tiles - 1))),
            pl.BlockSpec((None, d, tn), lambda i, j: (layer, 0, jnp.maximum(j - a_tiles, 0))),
            pl.BlockSpec((1, tn), lambda i, j: (0, j)),
            pl.BlockSpec((None, d, LANES), lambda i, j: (layer, 0, 0)),
        ],
        out_specs=[
            pl.BlockSpec((tm, tn), lambda i, j: (i, j)),
            pl.BlockSpec((tm, LANES), lambda i, j: (i, 0)),
        ],
        out_shape=[
            jax.ShapeDtypeStruct((n, ca + cb), BF16),
            jax.ShapeDtypeStruct((n, LANES), F32),
        ],
        scratch_shapes=[pltpu.VMEM((tm, d), BF16)],
        compiler_params=_params("parallel", "arbitrary"),
        name="in_proj",
    )(x2, gain, wa, wb, colgain, wf)


def _forget_cum_kernel(f_ref, b_ref, selq_ref, selk_ref, c_ref, qa_ref, ka_ref, carry_ref, *, n_heads):
    @pl.when(pl.program_id(1) == 0)
    def _():
        carry_ref[...] = jnp.zeros_like(carry_ref)

    z = f_ref[0] + b_ref[...]
    v = jnp.minimum(z, 0.0) - jnp.log1p(jnp.exp(-jnp.abs(z)))
    rows = v.shape[0]
    row = lax.broadcasted_iota(jnp.int32, v.shape, 0)
    shift = 1
    while shift < rows:
        v = v + jnp.where(row >= shift, pltpu.roll(v, shift, axis=0), 0.0)
        shift *= 2
    v = v + carry_ref[...]
    c_ref[0] = v
    carry_ref[...] = v[rows - 1:rows, :]

    lane = lax.broadcasted_iota(jnp.int32, v.shape, 1)
    heads_only = lane < n_heads
    c1 = v.astype(BF16).astype(F32)
    c2 = (v - c1).astype(BF16).astype(F32)
    c3 = v - c1 - c2
    packed = (jnp.where(heads_only, c1, 0.0)
              + pltpu.roll(jnp.where(heads_only, c2, 0.0), n_heads, axis=1)
              + pltpu.roll(jnp.where(heads_only, c3, 0.0), 2 * n_heads, axis=1)
              + jnp.where(lane == 3 * n_heads, 1.0, 0.0)).astype(BF16)
    qa_ref[0] = jnp.dot(packed, selq_ref[...], preferred_element_type=F32).astype(BF16)
    ka_ref[0] = jnp.dot(packed, selk_ref[...], preferred_element_type=F32).astype(BF16)


def _forget_selectors(n_heads, head_dim):
    selq = np.zeros((LANES, n_heads * head_dim), np.float32)
    selk = np.zeros((LANES, n_heads * head_dim), np.float32)
    one = 3 * n_heads
    for h in range(n_heads):
        for term in range(3):
            selq[term * n_heads + h, h * head_dim + term] = 1.0
            selk[term * n_heads + h, h * head_dim + 3 + term] = -1.0
            selq[one, h * head_dim + 3 + term] = 1.0
            selk[one, h * head_dim + term] = 1.0
    return jnp.asarray(selq, BF16), jnp.asarray(selk, BF16)


def _forget_cum(f3, bias, *, n_heads, head_dim):
    b, s, _ = f3.shape
    tc = _tile(s, 1024)
    wide = n_heads * head_dim
    assert 3 * n_heads < LANES and head_dim >= 6
    selq, selk = _forget_selectors(n_heads, head_dim)
    kern = functools.partial(_forget_cum_kernel, n_heads=n_heads)
    return pl.pallas_call(
        kern,
        grid=(b, s // tc),
        in_specs=[
            pl.BlockSpec((1, tc, LANES), lambda i, t: (i, t, 0)),
            pl.BlockSpec((1, LANES), lambda i, t: (0, 0)),
            pl.BlockSpec((LANES, wide), lambda i, t: (0, 0)),
            pl.BlockSpec((LANES, wide), lambda i, t: (0, 0)),
        ],
        out_specs=[
            pl.BlockSpec((1, tc, LANES), lambda i, t: (i, t, 0)),
            pl.BlockSpec((1, tc, wide), lambda i, t: (i, t, 0)),
            pl.BlockSpec((1, tc, wide), lambda i, t: (i, t, 0)),
        ],
        out_shape=[
            jax.ShapeDtypeStruct(f3.shape, F32),
            jax.ShapeDtypeStruct((b, s, wide), BF16),
            jax.ShapeDtypeStruct((b, s, wide), BF16),
        ],
        scratch_shapes=[pltpu.VMEM((1, LANES), F32)],
        compiler_params=_params("parallel", "arbitrary"),
        name="forget_cum",
    )(f3, bias, selq, selk)


def _fox_kernel(cedge_ref, qk_bound_ref, q_ref, qa_ref, k_ref, ka_ref, v_ref, cq_ref, o_ref,
                m_ref, l_ref, acc_ref, go_ref, *, blk, kblk, hd, heads):
    bi, g, qi = pl.program_id(0), pl.program_id(1), pl.program_id(2)
    first_head = (bi * pl.num_programs(1) + g) * heads
    cols = [slice(h * hd, (h + 1) * hd) for h in range(heads)]
    qs = [jnp.concatenate([q_ref[0, :, c], qa_ref[0, :, c]], axis=1) for c in cols]

    def blocks(h, kbs, state):
        m, l, acc = state
        for kb, diag in kbs:
            start = pl.multiple_of(kb * kblk, kblk)
            k = jnp.concatenate([k_ref[0, pl.ds(start, kblk), cols[h]], ka_ref[0, pl.ds(start, kblk), cols[h]]],
                                axis=1)
            s = lax.dot_general(k, qs[h], _NT, preferred_element_type=F32)
            yield
            if diag is not None:
                s = jnp.where(lax.broadcasted_iota(jnp.int32, s.shape, 0) + diag
                              <= lax.broadcasted_iota(jnp.int32, s.shape, 1), s, MASK_VALUE)
            m_new = jnp.maximum(m, jnp.max(s, axis=0, keepdims=True))
            alpha = jnp.exp(m - m_new)
            p = jnp.exp(s - m_new)
            l = alpha * l + jnp.sum(p, axis=0, keepdims=True)
            p_b = p.astype(BF16)
            m = m_new
            yield
            v = v_ref[0, pl.ds(start, kblk), cols[h]]
            acc = alpha * acc + lax.dot_general(v, p_b, _TN, preferred_element_type=F32)
        return m, l, acc

    def more_left(h, kb, m):
        return (jnp.max(cq_ref[h] - m) - cedge_ref[first_head + h, kb] + qk_bound_ref[0] > -EXP_UNDERFLOW
                ).astype(jnp.int32)

    per_q = blk // kblk
    first_kb = qi * per_q
    zeros = jnp.zeros((1, blk), F32)
    init = (zeros + MASK_VALUE, zeros, jnp.zeros((hd, blk), F32))

    def load(h):
        return m_ref[h], l_ref[h], acc_ref[h]

    def store(h, state):
        m_ref[h], l_ref[h], acc_ref[h] = state

    first = _run_skewed([blocks(h, [(first_kb + j, j * kblk) for j in range(per_q)], init) for h in range(heads)])
    for h in range(heads):
        store(h, first[h])
        go_ref[h] = more_left(h, first_kb, first[h][0])

    def live():
        return tuple(go_ref[h] != 0 for h in range(heads))

    def cond(c):
        t, gos = c
        return jnp.logical_and(t <= first_kb, functools.reduce(jnp.logical_or, gos))

    def body(c):
        t, gos = c
        kb = first_kb - t
        all_live = functools.reduce(jnp.logical_and, gos)

        @pl.when(all_live)
        def _():
            state = [load(h) for h in range(heads)]
            flags = [more_left(h, kb, state[h][0]) for h in range(heads)]
            out = _run_skewed([blocks(h, [(kb, None)], state[h]) for h in range(heads)])
            for h in range(heads):
                store(h, out[h])
                go_ref[h] = flags[h]

        for h in range(heads):
            @pl.when(jnp.logical_and(jnp.logical_not(all_live), gos[h]))
            def _(h=h):
                state = load(h)
                flag = more_left(h, kb, state[0])
                store(h, _run_skewed([blocks(h, [(kb, None)], state)])[0])
                go_ref[h] = flag

        return t + 1, live()

    lax.while_loop(cond, body, (jnp.int32(1), live()))
    for h in range(heads):
        o_ref[0, :, cols[h]] = jnp.transpose(acc_ref[h] / l_ref[h]).astype(o_ref.dtype)


def _fox_attn(proj3, qa3, ka3, c_row, cedge, qk_bound, *, n_heads, head_dim, q_off, k_off, v_off, out_width):
    b, s, _ = proj3.shape
    kblk = s // cedge.shape[1]
    blk = _tile(s, FOX_Q_BLOCK)
    heads = _heads_per_step(FOX_HEADS_PER_STEP, n_heads, q_off, k_off, v_off)
    wide = heads * head_dim
    groups = n_heads // heads
    kern = functools.partial(_fox_kernel, blk=blk, kblk=kblk, hd=head_dim, heads=heads)
    grid_spec = pltpu.PrefetchScalarGridSpec(
        num_scalar_prefetch=2,
        grid=(b, groups, s // blk),
        in_specs=[
            pl.BlockSpec((1, blk, wide), lambda bi, g, qi, *_: (bi, qi, q_off // heads + g)),
            pl.BlockSpec((1, blk, wide), lambda bi, g, qi, *_: (bi, qi, g)),
            pl.BlockSpec((1, s, wide), lambda bi, g, qi, *_: (bi, 0, k_off // heads + g)),
            pl.BlockSpec((1, s, wide), lambda bi, g, qi, *_: (bi, 0, g)),
            pl.BlockSpec((1, s, wide), lambda bi, g, qi, *_: (bi, 0, v_off // heads + g)),
            pl.BlockSpec((heads, 1, blk), lambda bi, g, qi, *_: (bi * groups + g, 0, qi)),
        ],
        out_specs=pl.BlockSpec((1, blk, wide), lambda bi, g, qi, *_: (bi, qi, g)),
        scratch_shapes=[pltpu.VMEM((heads, 1, blk), F32), pltpu.VMEM((heads, 1, blk), F32),
                        pltpu.VMEM((heads, head_dim, blk), F32), pltpu.SMEM((heads,), jnp.int32)],
    )
    return pl.pallas_call(
        kern,
        grid_spec=grid_spec,
        out_shape=jax.ShapeDtypeStruct((b, s, out_width), BF16),
        compiler_params=_params("parallel", "parallel", "arbitrary"),
        name="fox_attn",
    )(cedge, qk_bound, proj3, qa3, proj3, ka3, proj3, c_row)


def _sb_kernel(q_ref, k_ref, v_ref, o_ref, *, blk, hd, heads):
    qi = pl.program_id(2)
    tri = (lax.broadcasted_iota(jnp.int32, (blk, blk), 0)
           > lax.broadcasted_iota(jnp.int32, (blk, blk), 1)).astype(BF16)
    qs = [q_ref[0, :, h * hd:(h + 1) * hd] for h in range(heads)]

    def block(h, kb, masked):
        start = pl.multiple_of(kb * blk, blk)
        k = k_ref[0, pl.ds(start, blk), h * hd:(h + 1) * hd]
        z = lax.dot_general(qs[h], k, _NT, preferred_element_type=F32)
        yield
        sp = jnp.maximum(z, 0.0) + jnp.log(1.0 + jnp.exp(-jnp.abs(z)))
        log_beta = z - sp
        if masked:
            allowed = (lax.broadcasted_iota(jnp.int32, z.shape, 1)
                       < lax.broadcasted_iota(jnp.int32, z.shape, 0))
            sp = jnp.where(allowed, sp, 0.0)
        sp_sum = jnp.sum(sp, axis=-1, keepdims=True)
        sp_b = sp.astype(BF16)
        yield
        inner = jnp.dot(sp_b, tri, preferred_element_type=F32)
        yield
        w = jnp.exp(log_beta - inner)
        if masked:
            w = jnp.where(allowed, w, 0.0)
        w_b = w.astype(BF16)
        yield
        v = v_ref[0, pl.ds(start, blk), h * hd:(h + 1) * hd]
        return sp_sum, jnp.dot(w_b, v, preferred_element_type=F32)

    def alive(rights):
        return functools.reduce(jnp.maximum, [jnp.max(jnp.exp(-r)) for r in rights]) > 0.0

    has_left = (qi > 0).astype(F32)
    left = jnp.maximum(qi - 1, 0)
    first = _run_skewed([blk_ for h in range(heads) for blk_ in (block(h, qi, True), block(h, left, False))])
    rights, accs = [], []
    for h in range(heads):
        (sum_d, pv_d), (sum_l, pv_l) = first[2 * h], first[2 * h + 1]
        accs.append(pv_d + (has_left * jnp.exp(-sum_d)) * pv_l)
        rights.append(sum_d + sum_l)
    rights, accs = tuple(rights), tuple(accs)

    def cond(c):
        t, go, _, _ = c
        return jnp.logical_and(t < qi, go)

    def body(c):
        t, _, rights, accs = c
        new = _run_skewed([block(h, qi - 1 - t, False) for h in range(heads)])
        accs = tuple(accs[h] + jnp.exp(-rights[h]) * new[h][1] for h in range(heads))
        rights = tuple(rights[h] + new[h][0] for h in range(heads))
        return t + 1, alive(rights), rights, accs

    _, _, _, accs = lax.while_loop(cond, body, (jnp.int32(1), alive(rights), rights, accs))
    for h in range(heads):
        o_ref[0, :, h * hd:(h + 1) * hd] = accs[h].astype(o_ref.dtype)


def _sb_attn(proj3, *, n_heads, head_dim, q_off, k_off, v_off, out_width):
    b, s, _ = proj3.shape
    blk = _tile(s, MXU_DIM)
    heads = _heads_per_step(SB_HEADS_PER_STEP, n_heads, q_off, k_off, v_off)
    wide = heads * head_dim
    kern = functools.partial(_sb_kernel, blk=blk, hd=head_dim, heads=heads)
    return pl.pallas_call(
        kern,
        grid=(b, n_heads // heads, s // blk),
        in_specs=[
            pl.BlockSpec((1, blk, wide), lambda bi, g, qi: (bi, qi, q_off // heads + g)),
            pl.BlockSpec((1, s, wide), lambda bi, g, qi: (bi, 0, k_off // heads + g), pipeline_mode=pl.Buffered(1)),
            pl.BlockSpec((1, s, wide), lambda bi, g, qi: (bi, 0, v_off // heads + g), pipeline_mode=pl.Buffered(1)),
        ],
        out_specs=pl.BlockSpec((1, blk, wide), lambda bi, g, qi: (bi, qi, g)),
        out_shape=jax.ShapeDtypeStruct((b, s, out_width), BF16),
        compiler_params=_params("parallel", "parallel", "arbitrary"),
        name="sb_attn",
    )(proj3, proj3, proj3)


def _out_proj_kernel(oa_ref, ob_ref, ga_ref, gb_ref, w_ref, x_ref, o_ref, *, row_chains):
    def rows(r0, nrows):
        sl = slice(r0, r0 + nrows)
        merged = jnp.concatenate([_rms(oa_ref[sl, :].astype(F32), ga_ref[...]).astype(BF16),
                                  _rms(ob_ref[sl, :].astype(F32), gb_ref[...]).astype(BF16)], axis=1)
        yield
        o_ref[sl, :] = x_ref[sl, :] + jnp.dot(merged, w_ref[...], preferred_element_type=F32)

    nrows = o_ref.shape[0] // row_chains
    _run_skewed([rows(c * nrows, nrows) for c in range(row_chains)])


def _out_proj(oa, ob, ga, gb, w, x2, *, layer):
    n, d = x2.shape
    wa, wb = oa.shape[1], ob.shape[1]
    tm = _tile(n, 512)
    row_chains = OUT_ROW_CHAINS if tm % (OUT_ROW_CHAINS * SUBLANES) == 0 else 1
    return pl.pallas_call(
        functools.partial(_out_proj_kernel, row_chains=row_chains),
        grid=(n // tm,),
        in_specs=[
            pl.BlockSpec((tm, wa), lambda i: (i, 0)),
            pl.BlockSpec((tm, wb), lambda i: (i, 0)),
            pl.BlockSpec((1, wa), lambda i: (0, 0)),
            pl.BlockSpec((1, wb), lambda i: (0, 0)),
            pl.BlockSpec((None, wa + wb, d), lambda i: (layer, 0, 0)),
            pl.BlockSpec((tm, d), lambda i: (i, 0)),
        ],
        out_specs=pl.BlockSpec((tm, d), lambda i: (i, 0)),
        out_shape=jax.ShapeDtypeStruct((n, d), F32),
        compiler_params=_params("parallel"),
        name="out_proj",
    )(oa, ob, ga, gb, w, x2)


def _conv_ffn_kernel(x_ref, xp_ref, g_ref, wg_ref, wv_ref, cwg_ref, cwv_ref, cbg_ref, cbv_ref, wd_ref,
                     o_ref, h_ref, ug_ref, uv_ref, *, tiles_per_seq, row_chains):
    i = pl.program_id(0)
    tm = x_ref.shape[0]
    halo = xp_ref.shape[1]

    def conv(u_ref, nrows, cw_ref, cb_ref):
        cw = cw_ref[...]
        return (cw[2:3, :] * u_ref[halo:halo + nrows, :] + cw[1:2, :] * u_ref[halo - 1:halo - 1 + nrows, :]
                + cw[0:1, :] * u_ref[halo - 2:halo - 2 + nrows, :] + cb_ref[...])

    def rows(c, r0, nrows, first):
        if first:
            xs = x_ref[r0:r0 + nrows, :]
            h_ref[halo + r0:halo + r0 + nrows, :] = _rms(xs, g_ref[...]).astype(BF16)
            o_ref[r0:r0 + nrows, :] = xs
            yield
        h = h_ref[r0:r0 + halo + nrows, :]
        ug_ref[c] = jnp.dot(h, wg_ref[...], preferred_element_type=F32)
        uv_ref[c] = jnp.dot(h, wv_ref[...], preferred_element_type=F32)
        yield
        gate = conv(ug_ref.at[c], nrows, cwg_ref, cbg_ref)
        val = conv(uv_ref.at[c], nrows, cwv_ref, cbv_ref)
        act = (gate * (1.0 / (1.0 + jnp.exp(-gate))) * val).astype(BF16)
        yield
        o_ref[r0:r0 + nrows, :] += jnp.dot(act, wd_ref[...], preferred_element_type=F32)

    nrows = tm // row_chains
    j = pl.program_id(1)

    @pl.when(j == 0)
    def _():
        keep = (i % tiles_per_seq != 0).astype(F32)
        h_ref[:halo, :] = _rms(xp_ref[0] * keep, g_ref[...]).astype(BF16)
        _run_skewed([rows(c, c * nrows, nrows, True) for c in range(row_chains)])

    @pl.when(j > 0)
    def _():
        _run_skewed([rows(c, c * nrows, nrows, False) for c in range(row_chains)])


def _conv_ffn(x2, gain, w_up, conv_w, conv_b, w_down, *, layer, seq):
    n, d = x2.shape
    f = w_down.shape[1]
    tm = _tile(seq, 1024)
    tn = _tile(f, 512)
    halo = SUBLANES
    fb = f // tn
    xp = x2.reshape(n // halo, halo, d)
    row_chains = FFN_ROW_CHAINS if tm % (FFN_ROW_CHAINS * SUBLANES) == 0 else 1
    kern = functools.partial(_conv_ffn_kernel, tiles_per_seq=seq // tm, row_chains=row_chains)
    return pl.pallas_call(
        kern,
        grid=(n // tm, fb),
        in_specs=[
            pl.BlockSpec((tm, d), lambda i, j: (i, 0), pipeline_mode=pl.Buffered(1)),
            pl.BlockSpec((1, halo, d), lambda i, j: (jnp.maximum(i * (tm // halo) - 1, 0), 0, 0)),
            pl.BlockSpec((1, d), lambda i, j: (0, 0)),
            pl.BlockSpec((None, d, tn), lambda i, j: (layer, 0, j)),
            pl.BlockSpec((None, d, tn), lambda i, j: (layer, 0, j + fb)),
            pl.BlockSpec((conv_w.shape[0], tn), lambda i, j: (0, j)),
            pl.BlockSpec((conv_w.shape[0], tn), lambda i, j: (0, j + fb)),
            pl.BlockSpec((1, tn), lambda i, j: (0, j)),
            pl.BlockSpec((1, tn), lambda i, j: (0, j + fb)),
            pl.BlockSpec((None, tn, d), lambda i, j: (layer, j, 0)),
        ],
        out_specs=pl.BlockSpec((tm, d), lambda i, j: (i, 0)),
        out_shape=jax.ShapeDtypeStruct((n, d), F32),
        scratch_shapes=[pltpu.VMEM((halo + tm, d), BF16),
                        pltpu.VMEM((row_chains, halo + tm // row_chains, tn), F32),
                        pltpu.VMEM((row_chains, halo + tm // row_chains, tn), F32)],
        compiler_params=_params("parallel", "arbitrary"),
        name="conv_ffn",
    )(x2, xp, gain, w_up, w_up, conv_w, conv_w, conv_b, conv_b, w_down)


def kernel(x, attn_norm, w_in, b_forget, q_norm, k_norm, out_norm_fox, out_norm_sb, w_out, ffn_norm,
           w_up, conv_w, conv_b, w_down):
    batch, seq, d = x.shape
    depth = w_in.shape[0]
    hf = b_forget.shape[1]
    hd = q_norm.shape[1]
    wf_ = out_norm_fox.shape[1]
    ws_ = out_norm_sb.shape[1]
    hs = ws_ // hd
    n = batch * seq
    scale = hd ** -0.5
    fox_blk = _tile(seq, FOX_K_BLOCK)
    assert w_in.shape[2] == 3 * wf_ + hf + 3 * ws_ and hf * hd == wf_ and hf <= LANES

    w_in_b = w_in.astype(BF16)
    w_fox = w_in_b[:, :, :3 * wf_]
    w_sb = w_in_b[:, :, 3 * wf_ + hf:]
    w_forget = jnp.pad(w_in_b[:, :, 3 * wf_:3 * wf_ + hf], ((0, 0), (0, 0), (0, LANES - hf)))
    b_pad = jnp.pad(b_forget, ((0, 0), (0, LANES - hf)))
    ones_f = jnp.ones((depth, wf_), F32)
    ones_s = jnp.ones((depth, ws_), F32)
    colgain = jnp.concatenate([jnp.tile(q_norm * scale, (1, hf)), jnp.tile(k_norm, (1, hf)), ones_f,
                               ones_s * scale, ones_s, ones_s], axis=-1)
    w_out_b = w_out.astype(BF16)
    w_up_b = w_up.astype(BF16)
    w_down_b = w_down.astype(BF16)

    x2 = x.reshape(n, d)
    for l in range(depth):
        proj, f = _in_proj(x2, attn_norm[l][None], w_fox, w_sb, colgain[l][None], w_forget, layer=l,
                           norm_cols=2 * wf_, head_dim=hd)
        c, qa, ka = _forget_cum(f.reshape(batch, seq, LANES), b_pad[l][None], n_heads=hf, head_dim=hd)
        c_heads = jnp.swapaxes(c[:, :, :hf], 1, 2).reshape(batch * hf, seq)
        c_edge = c_heads[:, ::fox_blk]
        qk_bound = (1.01 * hd * scale * jnp.max(jnp.abs(q_norm[l])) * jnp.max(jnp.abs(k_norm[l]))).reshape(1)
        proj3 = proj.reshape(batch, seq, proj.shape[1])
        out_a = _fox_attn(proj3, qa, ka, c_heads[:, None, :], c_edge, qk_bound, n_heads=hf, head_dim=hd,
                          q_off=0, k_off=hf, v_off=2 * hf, out_width=wf_)
        out_b = _sb_attn(proj3, n_heads=hs, head_dim=hd, q_off=3 * hf, k_off=3 * hf + hs,
                         v_off=3 * hf + 2 * hs, out_width=ws_)
        x2 = _out_proj(out_a.reshape(n, wf_), out_b.reshape(n, ws_), out_norm_fox[l][None],
                       out_norm_sb[l][None], w_out_b, x2, layer=l)
        x2 = _conv_ffn(x2, ffn_norm[l][None], w_up_b, conv_w[l], conv_b[l][None], w_down_b, layer=l, seq=seq)
    return x2.reshape(batch, seq, d)
```
